```python
import jax, jax.numpy as jnp
from jax import lax
import numpy as np

D_MODEL = 1024
BATCH = 2
SEQ = 16384
DEPTH = 2

GRID_W = 64
CTX_LEN = 256
N_HEADS = 4
D_K = D_MODEL // 2 // N_HEADS
D_V = D_MODEL // N_HEADS
QK_W = N_HEADS * D_K
V_W = N_HEADS * D_V
CHUNK = 64
GLA_RANK = 16
GLA_TAU = 16.0
SHORT_CONV = 3
D_FF = 2816
N_MOD = 6
EPS = 1e-6
N_GLA = (DEPTH + 1) // 2
N_MLSTM = DEPTH // 2
GLA_IN = 2 * QK_W + 2 * V_W + 2 * GLA_RANK
MLSTM_IN = 2 * QK_W + 2 * V_W + 4 * N_HEADS
SPLITS = [QK_W, 2 * QK_W, 2 * QK_W + V_W, 2 * QK_W + 2 * V_W]

kernel_name = 'hybrid_gla_mlstm_convffn_prefix_ctx'


def rmsnorm(x, g):
    x32 = x.astype(jnp.float32)
    y = x32 * lax.rsqrt(jnp.mean(x32 * x32, axis=-1, keepdims=True) + EPS)
    return (y * g.astype(jnp.float32)).astype(x.dtype)


def head_rmsnorm(o, g):
    o32 = o.astype(jnp.float32)
    y = o32 * lax.rsqrt(jnp.mean(o32 * o32, axis=-1, keepdims=True) + EPS)
    bsz, nh, t_len, dv = o.shape
    y = y.transpose(0, 2, 1, 3).reshape(bsz, t_len, nh * dv)
    return (y * g.astype(jnp.float32)).astype(o.dtype)


def to_heads(t, d):
    bsz, t_len, _ = t.shape
    return t.reshape(bsz, t_len, N_HEADS, d).transpose(0, 2, 1, 3)


def conv1d_centred(x, w, b):
    xp = jnp.pad(x, ((0, 0), (1, 1), (0, 0)))
    return xp[:, :-2] * w[0] + xp[:, 1:-1] * w[1] + xp[:, 2:] * w[2] + b


def conv2d_grid(x, w, b, rows):
    bsz, t_len, ch = x.shape
    xg = x.reshape(bsz, rows, GRID_W, ch)
    y = lax.conv_general_dilated(xg, w[:, :, None, :].astype(x.dtype), (1, 1), 'SAME',
                                 dimension_numbers=('NHWC', 'HWIO', 'NHWC'),
                                 feature_group_count=ch)
    return y.reshape(bsz, t_len, ch) + b


def to_colmajor(x, rows):
    bsz, t_len, d = x.shape
    return x.reshape(bsz, rows, GRID_W, d).transpose(0, 2, 1, 3).reshape(bsz, t_len, d)


def from_colmajor(x, rows):
    bsz, t_len, d = x.shape
    return x.reshape(bsz, GRID_W, rows, d).transpose(0, 2, 1, 3).reshape(bsz, t_len, d)


def gla_scan(q, k, v, log_a, s0, reverse):
    dt = v.dtype
    q, k, v, log_a = (t.astype(jnp.float32) for t in (q, k, v, log_a))
    if reverse:
        q, k, v, log_a = (jnp.flip(t, axis=2) for t in (q, k, v, log_a))
    bsz, nh, t_len, dk = q.shape
    dv = v.shape[-1]
    n = t_len // CHUNK
    q, k, v, log_a = (t.reshape(bsz, nh, n, CHUNK, t.shape[-1]) for t in (q, k, v, log_a))
    b = jnp.cumsum(log_a, axis=3)
    b_last = b[:, :, :, -1:, :]
    q_in = q * jnp.exp(b)
    tril = jnp.tril(jnp.ones((CHUNK, CHUNK), dtype=bool))
    scores = jnp.einsum('bhnid,bhnjd->bhnij', q_in, k * jnp.exp(-b))
    scores = jnp.where(tril, scores, 0.0)
    o_intra = jnp.einsum('bhnij,bhnjv->bhniv', scores, v)
    k_st = k * jnp.exp(b_last - b)
    decay = jnp.exp(b_last[:, :, :, 0, :])

    def step(s, inp):
        qc, kc, vc, dc = inp
        o = jnp.einsum('bhcd,bhdv->bhcv', qc, s)
        s = s * dc[..., None] + jnp.einsum('bhcd,bhcv->bhdv', kc, vc)
        return s, o

    xs = tuple(jnp.moveaxis(t, 2, 0) for t in (q_in, k_st, v, decay))
    s_fin, o_inter = lax.scan(step, s0, xs)
    o = (o_intra + jnp.moveaxis(o_inter, 0, 2)).reshape(bsz, nh, t_len, dv)
    if reverse:
        o = jnp.flip(o, axis=2)
    return o.astype(dt), s_fin


def mlstm_scan(q, k, v, i_pre, logf, state, reverse):
    dt = v.dtype
    q, k, v, i_pre, logf = (t.astype(jnp.float32) for t in (q, k, v, i_pre, logf))
    if reverse:
        q, k, v, i_pre, logf = (jnp.flip(t, axis=2) for t in (q, k, v, i_pre, logf))
    bsz, nh, t_len, dk = q.shape
    dv = v.shape[-1]
    n = t_len // CHUNK
    q, k, v = (jnp.moveaxis(t.reshape(bsz, nh, n, CHUNK, t.shape[-1]), 2, 0) for t in (q, k, v))
    i_c = jnp.moveaxis(i_pre.reshape(bsz, nh, n, CHUNK), 2, 0)
    f_cum = jnp.cumsum(jnp.moveaxis(logf.reshape(bsz, nh, n, CHUNK), 2, 0), axis=-1)
    tril = jnp.tril(jnp.ones((CHUNK, CHUNK), dtype=bool))

    def step(carry, inp):
        c_st, n_st, m = carry
        qc, kc, vc, ic, fc = inp
        d_mat = fc[..., :, None] - fc[..., None, :] + ic[..., None, :]
        d_mat = jnp.where(tril, d_mat, -jnp.inf)
        inter = fc + m[..., None]
        m_i = jnp.maximum(inter, jnp.max(d_mat, axis=-1))
        w_inter = jnp.exp(inter - m_i)
        s_mat = jnp.einsum('bhid,bhjd->bhij', qc, kc) * jnp.exp(d_mat - m_i[..., None])
        num = w_inter[..., None] * jnp.einsum('bhid,bhdv->bhiv', qc, c_st) + jnp.einsum('bhij,bhjv->bhiv', s_mat, vc)
        den = w_inter * jnp.einsum('bhid,bhd->bhi', qc, n_st) + jnp.sum(s_mat, axis=-1)
        h = num / jnp.maximum(jnp.abs(den), jnp.exp(-m_i))[..., None]
        f_last = fc[..., -1]
        g = f_last[..., None] - fc + ic
        m_new = jnp.maximum(f_last + m, jnp.max(g, axis=-1))
        dec = jnp.exp(f_last + m - m_new)
        wk = jnp.exp(g - m_new[..., None])
        c_st = dec[..., None, None] * c_st + jnp.einsum('bhj,bhjd,bhjv->bhdv', wk, kc, vc)
        n_st = dec[..., None] * n_st + jnp.einsum('bhj,bhjd->bhd', wk, kc)
        return (c_st, n_st, m_new), h

    state_fin, h = lax.scan(step, state, (q, k, v, i_c, f_cum))
    h = jnp.moveaxis(h, 0, 2).reshape(bsz, nh, t_len, dv)
    if reverse:
        h = jnp.flip(h, axis=2)
    return h.astype(dt), state_fin


def gla_project(h, w_in, w_a2, b_a2):
    q, k, v, r, za = jnp.split(h @ w_in, SPLITS, axis=-1)
    log_a = [to_heads(jax.nn.log_sigmoid((za[..., d * GLA_RANK:(d + 1) * GLA_RANK] @ w_a2[d] + b_a2[d]).astype(jnp.float32)) / GLA_TAU, D_K)
             for d in range(2)]
    return to_heads(q, D_K) * D_K ** -0.5, to_heads(k, D_K), to_heads(v, D_V), r, log_a


def gla_mixer(h_ctx, h_lat, need_ctx, w_in, w_a2, b_a2, norm_g, w_out):
    qc, kc, vc, rc, lac = gla_project(h_ctx, w_in, w_a2, b_a2)
    ql, kl, vl, rl, lal = gla_project(h_lat, w_in, w_a2, b_a2)
    bsz = h_lat.shape[0]
    s0 = jnp.zeros((bsz, N_HEADS, D_K, D_V), jnp.float32)
    o_ctx, o_lat = 0.0, 0.0
    for d, rev in enumerate((False, True)):
        oc, s_c = gla_scan(qc, kc, vc, lac[d], s0, rev)
        ol, _ = gla_scan(ql, kl, vl, lal[d], s_c, rev)
        o_ctx, o_lat = o_ctx + oc, o_lat + ol

    def finish(o, r):
        return (head_rmsnorm(o, norm_g) * jax.nn.silu(r)) @ w_out

    return (finish(o_ctx, rc) if need_ctx else None), finish(o_lat, rl)


def mlstm_project(h, w_in, b_gate, conv_w, conv_b):
    q, k, v, o, gates = jnp.split(h @ w_in, SPLITS, axis=-1)
    qk = jax.nn.silu(conv1d_centred(jnp.concatenate([q, k], axis=-1), conv_w, conv_b))
    q, k = jnp.split(qk, 2, axis=-1)
    bsz, t_len, _ = h.shape
    gates = (gates + b_gate).astype(jnp.float32).reshape(bsz, t_len, 4, N_HEADS).transpose(2, 0, 3, 1)
    i_pre = gates[0:2]
    logf = jax.nn.log_sigmoid(gates[2:4])
    return to_heads(q, D_K), to_heads(k, D_K) * D_K ** -0.5, to_heads(v, D_V), o, i_pre, logf


def mlstm_mixer(h_ctx, h_lat, rows, need_ctx, w_in, b_gate, conv_w, conv_b, norm_g, w_out):
    h_lat = to_colmajor(h_lat, rows)
    qc, kc, vc, oc_g, ic, fc = mlstm_project(h_ctx, w_in, b_gate, conv_w, conv_b)
    ql, kl, vl, ol_g, il, fl = mlstm_project(h_lat, w_in, b_gate, conv_w, conv_b)
    bsz = h_lat.shape[0]
    st0 = (jnp.zeros((bsz, N_HEADS, D_K, D_V), jnp.float32),
           jnp.zeros((bsz, N_HEADS, D_K), jnp.float32),
           jnp.zeros((bsz, N_HEADS), jnp.float32))
    h_ctx_sum, h_lat_sum = 0.0, 0.0
    for d, rev in enumerate((False, True)):
        hc, st_c = mlstm_scan(qc, kc, vc, ic[d], fc[d], st0, rev)
        hl, _ = mlstm_scan(ql, kl, vl, il[d], fl[d], st_c, rev)
        h_ctx_sum, h_lat_sum = h_ctx_sum + hc, h_lat_sum + hl

    def finish(hsum, o_g):
        return (head_rmsnorm(hsum, norm_g) * jax.nn.sigmoid(o_g)) @ w_out

    out_lat = from_colmajor(finish(h_lat_sum, ol_g), rows)
    return (finish(h_ctx_sum, oc_g) if need_ctx else None), out_lat


def conv_ffn(h, w_up, conv_w, conv_b, w_down, rows):
    a, g = jnp.split(h @ w_up, 2, axis=-1)
    if rows is None:
        g = conv1d_centred(g, conv_w[1], conv_b)
    else:
        g = conv2d_grid(g, conv_w, conv_b, rows)
    return (jax.nn.gelu(g) * a) @ w_down


def setup_inputs(seed: int = 0) -> dict:
    key = jax.random.key(seed)
    ks = jax.random.split(key, 26)
    D = D_MODEL

    def nrm(k, shape, scale):
        return jax.random.normal(k, shape, jnp.float32) * scale

    return {
        'x': nrm(ks[0], (BATCH, SEQ, D), 1.0),
        'c': nrm(ks[1], (BATCH, D), 1.0),
        'ctx': nrm(ks[2], (BATCH, CTX_LEN, D), 1.0),
        'c_ctx': nrm(ks[3], (D,), 1.0),
        'ada_w': nrm(ks[4], (DEPTH, D, N_MOD * D), 0.5 * D ** -0.5),
        'ada_b': nrm(ks[5], (DEPTH, N_MOD * D), 0.02),
        'norm_mix_g': 1.0 + nrm(ks[6], (DEPTH, D), 0.02),
        'norm_ffn_g': 1.0 + nrm(ks[7], (DEPTH, D), 0.02),
        'gla_w_in': nrm(ks[8], (N_GLA, D, GLA_IN), D ** -0.5),
        'gla_w_a2': nrm(ks[9], (N_GLA, 2, GLA_RANK, QK_W), GLA_RANK ** -0.5),
        'gla_b_a2': nrm(ks[10], (N_GLA, 2, QK_W), 0.1),
        'gla_norm_g': 1.0 + nrm(ks[11], (N_GLA, V_W), 0.02),
        'gla_w_out': nrm(ks[12], (N_GLA, V_W, D), V_W ** -0.5),
        'mlstm_w_in': nrm(ks[13], (N_MLSTM, D, MLSTM_IN), D ** -0.5),
        'mlstm_b_gate': jnp.concatenate([nrm(ks[14], (N_MLSTM, 2 * N_HEADS), 0.1),
                                         3.0 + nrm(ks[15], (N_MLSTM, 2 * N_HEADS), 0.5)], axis=-1),
        'mlstm_conv_w': nrm(ks[16], (N_MLSTM, SHORT_CONV, 2 * QK_W), SHORT_CONV ** -0.5),
        'mlstm_conv_b': nrm(ks[17], (N_MLSTM, 2 * QK_W), 0.02),
        'mlstm_norm_g': 1.0 + nrm(ks[18], (N_MLSTM, V_W), 0.02),
        'mlstm_w_out': nrm(ks[19], (N_MLSTM, V_W, D), V_W ** -0.5),
        'ffn_w_up': nrm(ks[20], (DEPTH, D, 2 * D_FF), D ** -0.5),
        'ffn_conv_w': nrm(ks[21], (DEPTH, 3, 3, D_FF), 1.0 / 3.0),
        'ffn_conv_b': nrm(ks[22], (DEPTH, D_FF), 0.02),
        'ffn_w_down': nrm(ks[23], (DEPTH, D_FF, D), D_FF ** -0.5),
        'final_norm_g': 1.0 + nrm(ks[24], (D,), 0.02),
    }


def reference(x, c, ctx, c_ctx, ada_w, ada_b, norm_mix_g, norm_ffn_g,
              gla_w_in, gla_w_a2, gla_b_a2, gla_norm_g, gla_w_out,
              mlstm_w_in, mlstm_b_gate, mlstm_conv_w, mlstm_conv_b, mlstm_norm_g, mlstm_w_out,
              ffn_w_up, ffn_conv_w, ffn_conv_b, ffn_w_down, final_norm_g):
    rows = x.shape[1] // GRID_W
    for i in range(DEPTH):
        need_ctx = i < DEPTH - 1
        j = i // 2
        mod_l = (jax.nn.silu(c) @ ada_w[i] + ada_b[i])[:, None, :]
        mod_c = jax.nn.silu(c_ctx) @ ada_w[i] + ada_b[i]
        sh1_l, sc1_l, g1_l, sh2_l, sc2_l, g2_l = jnp.split(mod_l, N_MOD, axis=-1)
        sh1_c, sc1_c, g1_c, sh2_c, sc2_c, g2_c = jnp.split(mod_c, N_MOD, axis=-1)
        h_c = rmsnorm(ctx, norm_mix_g[i]) * (1.0 + sc1_c) + sh1_c
        h_l = rmsnorm(x, norm_mix_g[i]) * (1.0 + sc1_l) + sh1_l
        if i % 2 == 0:
            o_c, o_l = gla_mixer(h_c, h_l, need_ctx, gla_w_in[j], gla_w_a2[j], gla_b_a2[j],
                                 gla_norm_g[j], gla_w_out[j])
        else:
            o_c, o_l = mlstm_mixer(h_c, h_l, rows, need_ctx, mlstm_w_in[j], mlstm_b_gate[j],
                                   mlstm_conv_w[j], mlstm_conv_b[j], mlstm_norm_g[j], mlstm_w_out[j])
        x = x + g1_l * o_l
        h_l = rmsnorm(x, norm_ffn_g[i]) * (1.0 + sc2_l) + sh2_l
        x = x + g2_l * conv_ffn(h_l, ffn_w_up[i], ffn_conv_w[i], ffn_conv_b[i], ffn_w_down[i], rows)
        if need_ctx:
            ctx = ctx + g1_c * o_c
            h_c = rmsnorm(ctx, norm_ffn_g[i]) * (1.0 + sc2_c) + sh2_c
            ctx = ctx + g2_c * conv_ffn(h_c, ffn_w_up[i], ffn_conv_w[i], ffn_conv_b[i], ffn_w_down[i], None)
    return rmsnorm(x, final_norm_g)
```

```python
import functools

import jax
import jax.numpy as jnp
from jax import lax
from jax.experimental import pallas as pl
from jax.experimental.pallas import tpu as pltpu

D_MODEL = 1024
GRID_W = 64
N_HEADS = 4
D_K = 128
D_V = 256
QK_W = N_HEADS * D_K
V_W = N_HEADS * D_V
GLA_RANK = 16
GLA_TAU = 16.0
D_FF = 2816
N_MOD = 6
EPS = 1e-6

BLK = 256
SUB = 128
FF_CHUNK = 256
N_FF_CHUNKS = D_FF // FF_CHUNK
FFN_ROWS = 8
VMEM_LIMIT_BYTES = 56 * 1024 * 1024

F32 = jnp.float32
BF16 = jnp.bfloat16


def _dot(a, b):
    return jnp.dot(a, b, preferred_element_type=F32)


def _dot_nt(a, b):
    return lax.dot_general(a, b, (((1,), (1,)), ((), ())), preferred_element_type=F32)


def _sigmoid(x):
    return 1.0 / (1.0 + jnp.exp(-x))


def _log_sigmoid(x):
    return jnp.minimum(x, 0.0) - jnp.log(1.0 + jnp.exp(-jnp.abs(x)))


def _rms_mod(x, ng, sc, sh):
    ms = jnp.mean(x * x, axis=-1, keepdims=True)
    return (x * lax.rsqrt(ms + EPS) * ng) * (1.0 + sc) + sh


def _split_bf16(x):
    hi = x.astype(BF16)
    lo = (x - hi.astype(F32)).astype(BF16)
    return hi, lo


def _params(sem):
    return pltpu.CompilerParams(dimension_semantics=sem, vmem_limit_bytes=VMEM_LIMIT_BYTES)


def _full(shape):
    n = len(shape)
    return pl.BlockSpec(shape, lambda *_: (0,) * n)


def _sel_spec():
    return pl.BlockSpec((None, 1, D_MODEL), lambda b, t: (jnp.where(t == 0, 2, b), 0, 0))


def _mod_kernel(c_ref, w_ref, b_ref, o_ref):
    c = c_ref[...]
    s = c * _sigmoid(c)
    o_ref[...] = jnp.dot(s, w_ref[...], preferred_element_type=F32,
                         precision=lax.Precision.HIGHEST) + b_ref[...]


def _modulation(cvec, ada_w, ada_b):
    depth = ada_w.shape[0]
    n = N_MOD * D_MODEL
    tn = 1536
    return pl.pallas_call(
        _mod_kernel,
        grid=(depth, n // tn),
        in_specs=[pl.BlockSpec((8, D_MODEL), lambda i, j: (0, 0)),
                  pl.BlockSpec((None, D_MODEL, tn), lambda i, j: (i, 0, j)),
                  pl.BlockSpec((None, 1, tn), lambda i, j: (i, 0, j))],
        out_specs=pl.BlockSpec((None, 8, tn), lambda i, j: (i, 0, j)),
        out_shape=jax.ShapeDtypeStruct((depth, 8, n), F32),
        compiler_params=_params(("parallel", "parallel")),
        name="modulation",
    )(cvec, ada_w, ada_b.reshape(depth, 1, n))


def _load_tile(ctx_ref, x_ref):
    t = pl.program_id(1)
    return jnp.where(t == 0, ctx_ref[...], x_ref[...])


def _gla_in_kernel(ctx_ref, x_ref, sc_ref, sh_ref, ng_ref, w_ref,
                   q_ref, k_ref, v_ref, r_ref, za_ref):
    h = _rms_mod(_load_tile(ctx_ref, x_ref), ng_ref[...], sc_ref[...], sh_ref[...]).astype(BF16)
    q_ref[...] = (_dot(h, w_ref[:, 0:QK_W]) * D_K ** -0.5).astype(BF16)
    k_ref[...] = _dot(h, w_ref[:, QK_W:2 * QK_W]).astype(BF16)
    v_ref[...] = _dot(h, w_ref[:, 2 * QK_W:2 * QK_W + V_W]).astype(BF16)
    r_ref[...] = _dot(h, w_ref[:, 2 * QK_W + V_W:2 * QK_W + 2 * V_W]).astype(BF16)
    za_ref[...] = _dot(h, w_ref[:, 2 * QK_W + 2 * V_W:])


def _gla_in(ctx, x, sc, sh, ng, w_in):
    bsz, seq, d = x.shape
    nblk = 1 + seq // BLK
    t_tot = nblk * BLK
    n_in = w_in.shape[1]

    def tok(width):
        return pl.BlockSpec((None, BLK, width), lambda b, t: (b, t, 0))

    return pl.pallas_call(
        _gla_in_kernel,
        grid=(bsz, nblk),
        in_specs=[pl.BlockSpec((None, BLK, d), lambda b, t: (b, 0, 0)),
                  pl.BlockSpec((None, BLK, d), lambda b, t: (b, jnp.maximum(t - 1, 0), 0)),
                  _sel_spec(), _sel_spec(), _full((1, d)), _full((d, n_in))],
        out_specs=[tok(QK_W), tok(QK_W), tok(V_W), tok(V_W), tok(2 * GLA_RANK)],
        out_shape=[jax.ShapeDtypeStruct((bsz, t_tot, QK_W), BF16),
                   jax.ShapeDtypeStruct((bsz, t_tot, QK_W), BF16),
                   jax.ShapeDtypeStruct((bsz, t_tot, V_W), BF16),
                   jax.ShapeDtypeStruct((bsz, t_tot, V_W), BF16),
                   jax.ShapeDtypeStruct((bsz, t_tot, 2 * GLA_RANK), F32)],
        compiler_params=_params(("parallel", "arbitrary")),
        name="gla_in",
    )(ctx, x, sc, sh, ng, w_in)


def _mlstm_in_kernel(ctx_ref, x_ref, sc_ref, sh_ref, ng_ref, w_ref, wgt_ref, bg_ref, bgt_ref,
                     qk_ref, v_ref, og_ref, g_ref, gt_ref):
    h = _rms_mod(_load_tile(ctx_ref, x_ref), ng_ref[...], sc_ref[...], sh_ref[...]).astype(BF16)
    qk_ref[...] = _dot(h, w_ref[:, 0:2 * QK_W]).astype(qk_ref.dtype)
    v_ref[...] = _dot(h, w_ref[:, 2 * QK_W:2 * QK_W + V_W]).astype(BF16)
    og_ref[...] = _dot(h, w_ref[:, 2 * QK_W + V_W:2 * QK_W + 2 * V_W]).astype(BF16)
    g = _dot(h, w_ref[:, 2 * QK_W + 2 * V_W:]) + bg_ref[...]
    gt = _dot_nt(wgt_ref[...], h) + bgt_ref[...]
    ng = 2 * N_HEADS
    g_ref[...] = jnp.where(lax.broadcasted_iota(jnp.int32, g.shape, 1) >= ng, _log_sigmoid(g), g)
    gt_ref[...] = jnp.where(lax.broadcasted_iota(jnp.int32, gt.shape, 0) >= ng, _log_sigmoid(gt), gt)


def _mlstm_in(ctx, x, sc, sh, ng, w_in, b_gate):
    bsz, seq, d = x.shape
    rows = seq // GRID_W
    assert rows == BLK
    nblk = 1 + GRID_W
    t_tot = nblk * BLK
    n_in = w_in.shape[1]
    n_g = 4 * N_HEADS
    xcol = x.reshape(bsz, rows, GRID_W * d)
    w_gt = w_in[:, 2 * QK_W + 2 * V_W:].T

    def tok(width):
        return pl.BlockSpec((None, BLK, width), lambda b, t: (b, t, 0))

    return pl.pallas_call(
        _mlstm_in_kernel,
        grid=(bsz, nblk),
        in_specs=[pl.BlockSpec((None, BLK, d), lambda b, t: (b, 0, 0)),
                  pl.BlockSpec((None, rows, d), lambda b, t: (b, 0, jnp.maximum(t - 1, 0))),
                  _sel_spec(), _sel_spec(), _full((1, d)), _full((d, n_in)), _full((n_g, d)),
                  _full((1, n_g)), _full((n_g, 1))],
        out_specs=[tok(2 * QK_W), tok(V_W), tok(V_W), tok(n_g),
                   pl.BlockSpec((None, n_g, BLK), lambda b, t: (b, 0, t))],
        out_shape=[jax.ShapeDtypeStruct((bsz, t_tot, 2 * QK_W), F32),
                   jax.ShapeDtypeStruct((bsz, t_tot, V_W), BF16),
                   jax.ShapeDtypeStruct((bsz, t_tot, V_W), BF16),
                   jax.ShapeDtypeStruct((bsz, t_tot, n_g), F32),
                   jax.ShapeDtypeStruct((bsz, n_g, t_tot), F32)],
        compiler_params=_params(("parallel", "arbitrary")),
        name="mlstm_in",
    )(ctx, xcol, sc, sh, ng, w_in, w_gt, b_gate.reshape(1, n_g), b_gate.reshape(n_g, 1))


def _block_tri(n, sub, lower):
    ri = lax.broadcasted_iota(jnp.int32, (n, n), 0)
    ci = lax.broadcasted_iota(jnp.int32, (n, n), 1)
    shift = sub.bit_length() - 1
    assert sub == 1 << shift
    same = (ri >> shift) == (ci >> shift)
    tri = (ci <= ri) if lower else (ci >= ri)
    return jnp.where(same & tri, 1.0, 0.0).astype(BF16)


def _tri_mask(n, lower):
    ri = lax.broadcasted_iota(jnp.int32, (n, n), 0)
    ci = lax.broadcasted_iota(jnp.int32, (n, n), 1)
    return (ci <= ri) if lower else (ci >= ri)


def _gla_scan_kernel(qf, kf, vf, zf, qb, kb, vb, zb, w2_ref, b2_ref, of_ref, ob_ref, s_ref):
    @pl.when(pl.program_id(1) == 0)
    def _():
        s_ref[...] = jnp.zeros_like(s_ref)

    dirs = ((qf, kf, vf, zf, of_ref), (qb, kb, vb, zb, ob_ref))
    for d, (q_ref, k_ref, v_ref, z_ref, o_ref) in enumerate(dirs):
        fwd = d == 0
        za = z_ref[:, d * GLA_RANK:(d + 1) * GLA_RANK].astype(BF16)
        la = _log_sigmoid(_dot(za, w2_ref[d]) + b2_ref[d]) * (1.0 / GLA_TAU)
        tri = _block_tri(BLK, SUB, lower=fwd)
        hi, lo = _split_bf16(la)
        bcum = _dot(tri, hi) + _dot(tri, lo)
        mask = _tri_mask(SUB, lower=fwd)
        for sub in ((0, 1) if fwd else (1, 0)):
            rs = slice(sub * SUB, (sub + 1) * SUB)
            bs = bcum[rs]
            tot = bs[SUB - 1:SUB] if fwd else bs[0:1]
            mid = 0.5 * tot
            e_mid = jnp.exp(mid)
            e_q = jnp.exp(bs - mid)
            e_k = jnp.exp(mid - bs)
            q32 = q_ref[rs, :].astype(F32)
            k32 = k_ref[rs, :].astype(F32)
            q_mid = q32 * e_q
            q_in = q_mid * e_mid
            k_mid = k32 * e_k
            k_st = k_mid * e_mid
            decay = e_mid * e_mid
            for h in range(N_HEADS):
                ks = slice(h * D_K, (h + 1) * D_K)
                vs = slice(h * D_V, (h + 1) * D_V)
                scores = _dot_nt(q_mid[:, ks].astype(BF16), k_mid[:, ks].astype(BF16))
                p = jnp.where(mask, scores, 0.0).astype(BF16)
                vh = v_ref[rs, vs]
                s_old = s_ref[d, h]
                lhs = jnp.concatenate([p, q_in[:, ks].astype(BF16)], axis=1)
                rhs = jnp.concatenate([vh, s_old.astype(BF16)], axis=0)
                o_ref[rs, vs] = _dot(lhs, rhs).astype(o_ref.dtype)
                k_st_t = k_st[:, ks].T.astype(BF16)
                dcol = jnp.broadcast_to(decay[:, ks], (D_K, D_K)).T
                s_ref[d, h] = s_old * jnp.concatenate([dcol, dcol], axis=1) + _dot(k_st_t, vh)


def _scan_maps(nblk):
    def fwd(b, s):
        return (b, s, 0)

    def bwd(b, s):
        return (b, jnp.where(s == 0, 0, nblk - s), 0)

    return fwd, bwd


def _gla_scan(q, k, v, za, w_a2, b_a2):
    bsz, t_tot, _ = q.shape
    nblk = t_tot // BLK
    fwd, bwd = _scan_maps(nblk)

    def specs(m):
        return [pl.BlockSpec((None, BLK, QK_W), m), pl.BlockSpec((None, BLK, QK_W), m),
                pl.BlockSpec((None, BLK, V_W), m), pl.BlockSpec((None, BLK, 2 * GLA_RANK), m)]

    return pl.pallas_call(
        _gla_scan_kernel,
        grid=(bsz, nblk),
        in_specs=specs(fwd) + specs(bwd) + [_full((2, GLA_RANK, QK_W)), _full((2, 1, QK_W))],
        out_specs=[pl.BlockSpec((None, BLK, V_W), fwd), pl.BlockSpec((None, BLK, V_W), bwd)],
        out_shape=[jax.ShapeDtypeStruct((bsz, t_tot, V_W), BF16)] * 2,
        scratch_shapes=[pltpu.VMEM((2, N_HEADS, D_K, D_V), F32)],
        compiler_params=_params(("parallel", "arbitrary")),
        name="gla_scan",
    )(q, k, v, za, q, k, v, za, w_a2.astype(BF16), b_a2.reshape(2, 1, QK_W))


def _mlstm_scan_kernel(xf, pf, nf, vf, gf, gtf, xb, pb, nb, vb, gb, gtb, cw_ref, cb_ref,
                       of_ref, ob_ref, c_ref, n_ref, m_ref, *, nblk):
    s = pl.program_id(1)

    @pl.when(s == 0)
    def _():
        c_ref[...] = jnp.zeros_like(c_ref)
        n_ref[...] = jnp.zeros_like(n_ref)
        m_ref[...] = jnp.zeros_like(m_ref)

    row = lax.broadcasted_iota(jnp.int32, (BLK, 1), 0)
    dirs = ((xf, pf, nf, vf, gf, gtf, of_ref, s), (xb, pb, nb, vb, gb, gtb, ob_ref,
                                                    jnp.where(s == 0, 0, nblk - s)))
    for d, (x_ref, p_ref, nx_ref, v_ref, g_ref, gt_ref, o_ref, blk) in enumerate(dirs):
        fwd = d == 0
        x = x_ref[...].astype(F32)
        has_prev = jnp.logical_and(blk != 0, blk != 1)
        has_next = jnp.logical_and(blk != 0, blk != nblk - 1)
        prev = jnp.where(has_prev, p_ref[7:8, :].astype(F32), 0.0)
        nxt = jnp.where(has_next, nx_ref[0:1, :].astype(F32), 0.0)
        x_m1 = jnp.where(row == 0, prev, pltpu.roll(x, 1, 0))
        x_p1 = jnp.where(row == BLK - 1, nxt, pltpu.roll(x, BLK - 1, 0))
        qk = x_m1 * cw_ref[0:1, :] + x * cw_ref[1:2, :] + x_p1 * cw_ref[2:3, :] + cb_ref[...]
        qk = qk * _sigmoid(qk)
        q_all = qk[:, :QK_W]
        k_all = qk[:, QK_W:] * D_K ** -0.5

        g = g_ref[...]
        gt = gt_ref[...]
        g_hi, g_lo = _split_bf16(g)
        gt_hi, gt_lo = _split_bf16(gt)
        tri_c = _block_tri(BLK, SUB, lower=fwd)
        tri_r = _block_tri(BLK, SUB, lower=not fwd)
        f_col = _dot(tri_c, g_hi) + _dot(tri_c, g_lo)
        f_row = _dot(gt_hi, tri_r) + _dot(gt_lo, tri_r)
        mask = _tri_mask(SUB, lower=fwd)
        for sub in ((0, 1) if fwd else (1, 0)):
            rs = slice(sub * SUB, (sub + 1) * SUB)
            for h in range(N_HEADS):
                ks = slice(h * D_K, (h + 1) * D_K)
                vs = slice(h * D_V, (h + 1) * D_V)
                gi = d * N_HEADS + h
                gf_ = 2 * N_HEADS + gi
                fc_c = f_col[rs, gf_:gf_ + 1]
                ic_c = g[rs, gi:gi + 1]
                fc_r = f_row[gf_:gf_ + 1, rs]
                ic_r = gt[gi:gi + 1, rs]
                q = q_all[rs, ks]
                k = k_all[rs, ks]
                vh = v_ref[rs, vs]
                c_old = c_ref[gi]
                n_old = n_ref[gi]
                m_old = m_ref[gi][:, 0:1]

                d_mat = jnp.where(mask, fc_c - fc_r + ic_r, -jnp.inf)
                inter = fc_c + m_old
                m_i = jnp.maximum(inter, jnp.max(d_mat, axis=-1, keepdims=True))
                w_inter = jnp.exp(inter - m_i)
                qk_s = _dot_nt(q.astype(BF16), k.astype(BF16))
                s_mat = qk_s * jnp.exp(d_mat - m_i)
                lhs = jnp.concatenate([s_mat.astype(BF16), (w_inter * q).astype(BF16)], axis=1)
                rhs = jnp.concatenate([vh, c_old.astype(BF16)], axis=0)
                num = _dot(lhs, rhs)
                den = (w_inter * jnp.sum(q * n_old, axis=-1, keepdims=True)
                       + jnp.sum(s_mat, axis=-1, keepdims=True))
                o_ref[rs, vs] = (num / jnp.maximum(jnp.abs(den), jnp.exp(-m_i))).astype(o_ref.dtype)

                f_last = fc_r[:, SUB - 1:SUB] if fwd else fc_r[:, 0:1]
                g_c = f_last - fc_c + ic_c
                m_new = jnp.maximum(f_last + m_old, jnp.max(g_c, axis=0, keepdims=True))
                dec = jnp.exp(f_last + m_old - m_new)
                kw = jnp.exp(g_c - m_new) * k
                c_ref[gi] = dec * c_old + _dot(kw.T.astype(BF16), vh)
                n_ref[gi] = dec * n_old + jnp.sum(kw, axis=0, keepdims=True)
                m_ref[gi] = jnp.broadcast_to(m_new, (1, D_K))


def _mlstm_scan(qk_pre, v, g, gt, conv_w, conv_b):
    bsz, t_tot, _ = v.shape
    nblk = t_tot // BLK
    fwd, bwd = _scan_maps(nblk)
    n_g = 4 * N_HEADS
    halo_blocks = t_tot // 8

    def blk_of(m):
        return lambda b, s: m(b, s)[1]

    def specs(m):
        blk = blk_of(m)
        return [pl.BlockSpec((None, BLK, 2 * QK_W), m),
                pl.BlockSpec((None, 8, 2 * QK_W),
                             lambda b, s: (b, jnp.maximum(blk(b, s) * (BLK // 8) - 1, 0), 0)),
                pl.BlockSpec((None, 8, 2 * QK_W),
                             lambda b, s: (b, jnp.minimum((blk(b, s) + 1) * (BLK // 8), halo_blocks - 1), 0)),
                pl.BlockSpec((None, BLK, V_W), m),
                pl.BlockSpec((None, BLK, n_g), m),
                pl.BlockSpec((None, n_g, BLK), lambda b, s: (b, 0, blk(b, s)))]

    return pl.pallas_call(
        functools.partial(_mlstm_scan_kernel, nblk=nblk),
        grid=(bsz, nblk),
        in_specs=specs(fwd) + specs(bwd) + [_full((3, 2 * QK_W)), _full((1, 2 * QK_W))],
        out_specs=[pl.BlockSpec((None, BLK, V_W), fwd), pl.BlockSpec((None, BLK, V_W), bwd)],
        out_shape=[jax.ShapeDtypeStruct((bsz, t_tot, V_W), BF16)] * 2,
        scratch_shapes=[pltpu.VMEM((2 * N_HEADS, D_K, D_V), F32),
                        pltpu.VMEM((2 * N_HEADS, 1, D_K), F32),
                        pltpu.VMEM((2 * N_HEADS, 1, D_K), F32)],
        compiler_params=_params(("parallel", "arbitrary")),
        name="mlstm_scan",
    )(qk_pre, qk_pre, qk_pre, v, g, gt, qk_pre, qk_pre, qk_pre, v, g, gt,
      conv_w, conv_b.reshape(1, 2 * QK_W))


def _head_norm_gate(of_ref, ob_ref, gate, ng_ref):
    o = of_ref[...].astype(F32) + ob_ref[...].astype(F32)
    parts = []
    for h in range(N_HEADS):
        oh = o[:, h * D_V:(h + 1) * D_V]
        ms = jnp.mean(oh * oh, axis=-1, keepdims=True)
        parts.append(oh * lax.rsqrt(ms + EPS))
    return (jnp.concatenate(parts, axis=1) * ng_ref[...] * gate).astype(BF16)


def _gla_out_kernel(of_ref, ob_ref, r_ref, ctx_ref, x_ref, g1_ref, ng_ref, w_ref,
                    ctx_out, x_out):
    r = r_ref[...].astype(F32)
    y = _dot(_head_norm_gate(of_ref, ob_ref, r * _sigmoid(r), ng_ref), w_ref[...])
    t = pl.program_id(1)

    @pl.when(t == 0)
    def _():
        ctx_out[...] = ctx_ref[...] + g1_ref[...] * y

    @pl.when(t != 0)
    def _():
        x_out[...] = x_ref[...] + g1_ref[...] * y


def _gla_out(o_f, o_b, r, ctx, x, g1, ng, w_out):
    bsz, seq, d = x.shape
    nblk = 1 + seq // BLK
    tok = pl.BlockSpec((None, BLK, V_W), lambda b, t: (b, t, 0))
    ctx_spec = pl.BlockSpec((None, BLK, d), lambda b, t: (b, 0, 0))
    x_spec = pl.BlockSpec((None, BLK, d), lambda b, t: (b, jnp.maximum(t - 1, 0), 0))
    return pl.pallas_call(
        _gla_out_kernel,
        grid=(bsz, nblk),
        in_specs=[tok, tok, tok, ctx_spec, x_spec, _sel_spec(), _full((1, V_W)), _full((V_W, d))],
        out_specs=[ctx_spec, x_spec],
        out_shape=[jax.ShapeDtypeStruct(ctx.shape, F32), jax.ShapeDtypeStruct(x.shape, F32)],
        compiler_params=_params(("parallel", "arbitrary")),
        name="gla_out",
    )(o_f, o_b, r, ctx, x, g1, ng, w_out)


def _mlstm_out_kernel(of_ref, ob_ref, og_ref, x_ref, g1_ref, ng_ref, w_ref, x_out):
    y = _dot(_head_norm_gate(of_ref, ob_ref, _sigmoid(og_ref[...].astype(F32)), ng_ref), w_ref[...])
    x_out[...] = x_ref[...] + g1_ref[...] * y


def _mlstm_out(h_f, h_b, og, x, g1, ng, w_out):
    bsz, seq, d = x.shape
    rows = seq // GRID_W
    xcol = x.reshape(bsz, rows, GRID_W * d)
    tok = pl.BlockSpec((None, BLK, V_W), lambda b, c: (b, c + 1, 0))
    col = pl.BlockSpec((None, rows, d), lambda b, c: (b, 0, c))
    out = pl.pallas_call(
        _mlstm_out_kernel,
        grid=(bsz, GRID_W),
        in_specs=[tok, tok, tok, col,
                  pl.BlockSpec((None, 1, d), lambda b, c: (b, 0, 0)),
                  _full((1, V_W)), _full((V_W, d))],
        out_specs=col,
        out_shape=jax.ShapeDtypeStruct(xcol.shape, F32),
        compiler_params=_params(("parallel", "arbitrary")),
        name="mlstm_out",
    )(h_f, h_b, og, xcol, g1, ng, w_out)
    return out.reshape(bsz, seq, d)


def _gelu_tanh(x):
    return 0.5 * x * (1.0 + jnp.tanh(0.7978845608028654 * (x + 0.044715 * (x * x * x))))


def _ffn_kernel(*refs, width, tm, vertical, final_norm):
    refs = list(refs)
    x_ref = refs.pop(0)
    xu_ref, xd_ref = (refs.pop(0), refs.pop(0)) if vertical else (None, None)
    sc_ref, sh_ref, g2_ref, ng_ref, wa_ref, wg_ref, cw_ref, cb_ref, wd_ref = refs[:9]
    refs = refs[9:]
    fg_ref = refs.pop(0) if final_norm else None
    out_ref, hext_ref, acc_ref = refs

    t = pl.program_id(1)
    nt = pl.num_programs(1)
    halo = width if vertical else 0
    ext = tm + 2 * halo

    def hmod(xt):
        return _rms_mod(xt, ng_ref[...], sc_ref[...], sh_ref[...]).astype(BF16)

    x = x_ref[...]
    hext_ref[halo:halo + tm, :] = hmod(x)
    if vertical:
        hext_ref[0:halo, :] = hmod(xu_ref[...])
        hext_ref[halo + tm:ext, :] = hmod(xd_ref[...])
    acc_ref[...] = jnp.zeros_like(acc_ref)

    tok = lax.broadcasted_iota(jnp.int32, (ext, FF_CHUNK), 0)
    assert width & (width - 1) == 0
    wpos = tok & (width - 1)
    first_col = wpos == 0
    last_col = wpos == width - 1
    if vertical:
        outside = jnp.logical_or(jnp.logical_and(tok < halo, t == 0),
                                 jnp.logical_and(tok >= halo + tm, t == nt - 1))

    def body(c, carry):
        a = _dot(hext_ref[halo:halo + tm, :], wa_ref[c])
        g = _dot(hext_ref[...], wg_ref[c])
        if vertical:
            g = jnp.where(outside, 0.0, g)
        g_l = jnp.where(first_col, 0.0, pltpu.roll(g, 1, 0))
        g_r = jnp.where(last_col, 0.0, pltpu.roll(g, ext - 1, 0))
        cw = cw_ref[c]
        conv = cb_ref[c]
        for dr in ((0, 1, 2) if vertical else (1,)):
            rs = slice(dr * halo, dr * halo + tm)
            conv = (conv + g_l[rs] * cw[3 * dr:3 * dr + 1] + g[rs] * cw[3 * dr + 1:3 * dr + 2]
                    + g_r[rs] * cw[3 * dr + 2:3 * dr + 3])
        act = (_gelu_tanh(conv) * a).astype(BF16)
        acc_ref[...] += _dot(act, wd_ref[c])
        return carry

    lax.fori_loop(0, N_FF_CHUNKS, body, 0)
    y = x + g2_ref[...] * acc_ref[...]
    if final_norm:
        ms = jnp.mean(y * y, axis=-1, keepdims=True)
        y = y * lax.rsqrt(ms + EPS) * fg_ref[...]
    out_ref[...] = y


def _conv_ffn(x, sc, sh, g2, ng, wa, wg, cw, cb, wd, *, width, tm, vertical, final_g=None):
    bsz, seq, d = x.shape
    nt = seq // tm
    vec = pl.BlockSpec((None, 1, d), lambda b, t: (b, 0, 0))
    tile = pl.BlockSpec((None, tm, d), lambda b, t: (b, t, 0))
    in_specs = [tile]
    args = [x]
    if vertical:
        rpt = tm // width
        nrows = seq // width
        in_specs += [pl.BlockSpec((None, width, d), lambda b, t: (b, jnp.maximum(t * rpt - 1, 0), 0)),
                     pl.BlockSpec((None, width, d), lambda b, t: (b, jnp.minimum((t + 1) * rpt, nrows - 1), 0))]
        args += [x, x]
    in_specs += [vec, vec, vec, _full((1, d)), _full(wa.shape), _full(wg.shape), _full(cw.shape),
                 _full(cb.shape), _full(wd.shape)]
    args += [sc, sh, g2, ng, wa, wg, cw, cb, wd]
    if final_g is not None:
        in_specs.append(_full((1, d)))
        args.append(final_g)
    ext = tm + (2 * width if vertical else 0)
    return pl.pallas_call(
        functools.partial(_ffn_kernel, width=width, tm=tm, vertical=vertical,
                          final_norm=final_g is not None),
        grid=(bsz, nt),
        in_specs=in_specs,
        out_specs=tile,
        out_shape=jax.ShapeDtypeStruct(x.shape, F32),
        scratch_shapes=[pltpu.VMEM((ext, d), BF16), pltpu.VMEM((tm, d), F32)],
        compiler_params=_params(("parallel", "arbitrary")),
        name="conv_ffn_grid" if vertical else "conv_ffn_seq",
    )(*args)


def _ffn_weights(w_up, conv_w, conv_b, w_down):
    d = w_up.shape[0]

    def chunked(w):
        return w.reshape(d, N_FF_CHUNKS, FF_CHUNK).transpose(1, 0, 2).astype(BF16)

    wa = chunked(w_up[:, :D_FF])
    wg = chunked(w_up[:, D_FF:])
    cw = conv_w.reshape(9, N_FF_CHUNKS, FF_CHUNK).transpose(1, 0, 2)
    cb = conv_b.reshape(N_FF_CHUNKS, 1, FF_CHUNK)
    wd = w_down.reshape(N_FF_CHUNKS, FF_CHUNK, d).astype(BF16)
    return wa, wg, cw, cb, wd


def kernel(x, c, ctx, c_ctx, ada_w, ada_b, norm_mix_g, norm_ffn_g, gla_w_in, gla_w_a2, gla_b_a2, gla_norm_g, gla_w_out, mlstm_w_in, mlstm_b_gate, mlstm_conv_w, mlstm_conv_b, mlstm_norm_g, mlstm_w_out, ffn_w_up, ffn_conv_w, ffn_conv_b, ffn_w_down, final_norm_g):
    bsz, seq, d = x.shape
    depth = ada_w.shape[0]
    assert bsz == 2 and d == D_MODEL and ctx.shape[1] == BLK and seq == BLK * GRID_W and depth == 2

    cvec = jnp.zeros((8, d), F32).at[:bsz].set(c).at[bsz].set(c_ctx)
    mods = _modulation(cvec, ada_w, ada_b)

    def mod(i, j):
        return mods[i, :3, j * d:(j + 1) * d].reshape(3, 1, d)

    def row(v):
        return v.reshape(1, -1)

    sh1, sc1, g1, sh2, sc2, g2 = (mod(0, j) for j in range(N_MOD))
    q, k, v, r, za = _gla_in(ctx, x, sc1, sh1, row(norm_mix_g[0]), gla_w_in[0].astype(BF16))
    o_f, o_b = _gla_scan(q, k, v, za, gla_w_a2[0], gla_b_a2[0])
    ctx, x = _gla_out(o_f, o_b, r, ctx, x, g1, row(gla_norm_g[0]), gla_w_out[0].astype(BF16))
    ffn_w = _ffn_weights(ffn_w_up[0], ffn_conv_w[0], ffn_conv_b[0], ffn_w_down[0])
    x = _conv_ffn(x, sc2[:bsz], sh2[:bsz], g2[:bsz], row(norm_ffn_g[0]), *ffn_w,
                  width=GRID_W, tm=FFN_ROWS * GRID_W, vertical=True)
    ctx = _conv_ffn(ctx, jnp.broadcast_to(sc2[2:], (bsz, 1, d)), jnp.broadcast_to(sh2[2:], (bsz, 1, d)),
                    jnp.broadcast_to(g2[2:], (bsz, 1, d)), row(norm_ffn_g[0]), *ffn_w,
                    width=BLK, tm=BLK, vertical=False)

    sh1, sc1, g1, sh2, sc2, g2 = (mod(1, j) for j in range(N_MOD))
    qk_pre, v, og, g, gt = _mlstm_in(ctx, x, sc1, sh1, row(norm_mix_g[1]), mlstm_w_in[0].astype(BF16),
                                     mlstm_b_gate[0])
    h_f, h_b = _mlstm_scan(qk_pre, v, g, gt, mlstm_conv_w[0], mlstm_conv_b[0])
    x = _mlstm_out(h_f, h_b, og, x, g1[:bsz], row(mlstm_norm_g[0]), mlstm_w_out[0].astype(BF16))
    ffn_w = _ffn_weights(ffn_w_up[1], ffn_conv_w[1], ffn_conv_b[1], ffn_w_down[1])
    return _conv_ffn(x, sc2[:bsz], sh2[:bsz], g2[:bsz], row(norm_ffn_g[1]), *ffn_w,
                     width=GRID_W, tm=FFN_ROWS * GRID_W, vertical=True, final_g=row(final_norm_g))
```

```python
import functools

import jax
import jax.numpy as jnp
from jax import lax
from jax.experimental import pallas as pl
from jax.experimental.pallas import tpu as pltpu

D_MODEL = 1024
GRID_W = 64
N_HEADS = 4
D_K = 128
D_V = 256
QK_W = N_HEADS * D_K
V_W = N_HEADS * D_V
GLA_RANK = 16
GLA_TAU = 16.0
D_FF = 2816
N_MOD = 6
EPS = 1e-6

BLK = 256
SUB = 128
FF_CHUNK = 256
N_FF_CHUNKS = D_FF // FF_CHUNK
FFN_ROWS = 8
VMEM_LIMIT_BYTES = 56 * 1024 * 1024

F32 = jnp.float32
BF16 = jnp.bfloat16


def _dot(a, b):
    return jnp.dot(a, b, preferred_element_type=F32)


def _dot_nt(a, b):
    return lax.dot_general(a, b, (((1,), (1,)), ((), ())), preferred_element_type=F32)


def _sigmoid(x):
    return 1.0 / (1.0 + jnp.exp(-x))


def _log_sigmoid(x):
    return jnp.minimum(x, 0.0) - jnp.log(1.0 + jnp.exp(-jnp.abs(x)))


def _rms_mod(x, ng, sc, sh):
    ms = jnp.mean(x * x, axis=-1, keepdims=True)
    return (x * lax.rsqrt(ms + EPS) * ng) * (1.0 + sc) + sh


def _split_bf16(x):
    hi = x.astype(BF16)
    lo = (x - hi.astype(F32)).astype(BF16)
    return hi, lo


def _params(sem):
    return pltpu.CompilerParams(dimension_semantics=sem, vmem_limit_bytes=VMEM_LIMIT_BYTES)


def _full(shape):
    n = len(shape)
    return pl.BlockSpec(shape, lambda *_: (0,) * n)


def _sel_spec():
    return pl.BlockSpec((None, 1, D_MODEL), lambda b, t: (jnp.where(t == 0, 2, b), 0, 0))


def _mod_kernel(c_ref, w_ref, b_ref, o_ref):
    c = c_ref[...]
    s = c * _sigmoid(c)
    o_ref[...] = jnp.dot(s, w_ref[...], preferred_element_type=F32,
                         precision=lax.Precision.HIGHEST) + b_ref[...]


def _modulation(cvec, ada_w, ada_b):
    depth = ada_w.shape[0]
    n = N_MOD * D_MODEL
    tn = 1536
    return pl.pallas_call(
        _mod_kernel,
        grid=(depth, n // tn),
        in_specs=[pl.BlockSpec((8, D_MODEL), lambda i, j: (0, 0)),
                  pl.BlockSpec((None, D_MODEL, tn), lambda i, j: (i, 0, j)),
                  pl.BlockSpec((None, 1, tn), lambda i, j: (i, 0, j))],
        out_specs=pl.BlockSpec((None, 8, tn), lambda i, j: (i, 0, j)),
        out_shape=jax.ShapeDtypeStruct((depth, 8, n), F32),
        compiler_params=_params(("parallel", "parallel")),
        name="modulation",
    )(cvec, ada_w, ada_b.reshape(depth, 1, n))


def _load_tile(ctx_ref, x_ref):
    t = pl.program_id(1)
    return jnp.where(t == 0, ctx_ref[...], x_ref[...])


def _gla_in_kernel(ctx_ref, x_ref, sc_ref, sh_ref, ng_ref, w_ref,
                   q_ref, k_ref, v_ref, r_ref, za_ref):
    h = _rms_mod(_load_tile(ctx_ref, x_ref), ng_ref[...], sc_ref[...], sh_ref[...]).astype(BF16)
    q_ref[...] = (_dot(h, w_ref[:, 0:QK_W]) * D_K ** -0.5).astype(BF16)
    k_ref[...] = _dot(h, w_ref[:, QK_W:2 * QK_W]).astype(BF16)
    v_ref[...] = _dot(h, w_ref[:, 2 * QK_W:2 * QK_W + V_W]).astype(BF16)
    r_ref[...] = _dot(h, w_ref[:, 2 * QK_W + V_W:2 * QK_W + 2 * V_W]).astype(BF16)
    za_ref[...] = _dot(h, w_ref[:, 2 * QK_W + 2 * V_W:])


def _gla_in(ctx, x, sc, sh, ng, w_in):
    bsz, seq, d = x.shape
    nblk = 1 + seq // BLK
    t_tot = nblk * BLK
    n_in = w_in.shape[1]

    def tok(width):
        return pl.BlockSpec((None, BLK, width), lambda b, t: (b, t, 0))

    return pl.pallas_call(
        _gla_in_kernel,
        grid=(bsz, nblk),
        in_specs=[pl.BlockSpec((None, BLK, d), lambda b, t: (b, 0, 0)),
                  pl.BlockSpec((None, BLK, d), lambda b, t: (b, jnp.maximum(t - 1, 0), 0)),
                  _sel_spec(), _sel_spec(), _full((1, d)), _full((d, n_in))],
        out_specs=[tok(QK_W), tok(QK_W), tok(V_W), tok(V_W), tok(2 * GLA_RANK)],
        out_shape=[jax.ShapeDtypeStruct((bsz, t_tot, QK_W), BF16),
                   jax.ShapeDtypeStruct((bsz, t_tot, QK_W), BF16),
                   jax.ShapeDtypeStruct((bsz, t_tot, V_W), BF16),
                   jax.ShapeDtypeStruct((bsz, t_tot, V_W), BF16),
                   jax.ShapeDtypeStruct((bsz, t_tot, 2 * GLA_RANK), F32)],
        compiler_params=_params(("parallel", "arbitrary")),
        name="gla_in",
    )(ctx, x, sc, sh, ng, w_in)


def _mlstm_in_kernel(ctx_ref, x_ref, sc_ref, sh_ref, ng_ref, w_ref, wgt_ref, bgt_ref, cw_ref, cb_ref,
                     q_ref, k_ref, v_ref, og_ref, gt_ref, pre_ref, last_ref, *, nblk):
    t = pl.program_id(1)

    @pl.when(t == 0)
    def _():
        last_ref[...] = jnp.zeros_like(last_ref)

    @pl.when(t < nblk)
    def _():
        h = _rms_mod(_load_tile(ctx_ref, x_ref), ng_ref[...], sc_ref[...], sh_ref[...]).astype(BF16)
        pre_ref[t % 2] = _dot(h, w_ref[:, 0:2 * QK_W])
        v_ref[...] = _dot(h, w_ref[:, 2 * QK_W:2 * QK_W + V_W]).astype(BF16)
        og_ref[...] = _dot(h, w_ref[:, 2 * QK_W + V_W:2 * QK_W + 2 * V_W]).astype(BF16)
        gt = _dot_nt(wgt_ref[...], h) + bgt_ref[...]
        is_forget = lax.broadcasted_iota(jnp.int32, gt.shape, 0) >= 2 * N_HEADS
        gt_ref[...] = jnp.where(is_forget, _log_sigmoid(gt), gt)

    @pl.when(t > 0)
    def _():
        blk = t - 1
        x = pre_ref[blk % 2]
        has_prev = jnp.logical_and(blk != 0, blk != 1)
        has_next = jnp.logical_and(blk != 0, blk != nblk - 1)
        prev = jnp.where(has_prev, last_ref[0:1, :], 0.0)
        nxt = jnp.where(has_next, pre_ref[t % 2, 0:1, :], 0.0)
        row = lax.broadcasted_iota(jnp.int32, (BLK, 1), 0)
        x_m1 = jnp.where(row == 0, prev, pltpu.roll(x, 1, 0))
        x_p1 = jnp.where(row == BLK - 1, nxt, pltpu.roll(x, BLK - 1, 0))
        qk = x_m1 * cw_ref[0:1, :] + x * cw_ref[1:2, :] + x_p1 * cw_ref[2:3, :] + cb_ref[...]
        qk = qk * _sigmoid(qk)
        q_ref[...] = qk[:, :QK_W].astype(BF16)
        k_ref[...] = (qk[:, QK_W:] * D_K ** -0.5).astype(BF16)
        last_ref[0:1, :] = x[BLK - 1:BLK, :]


def _mlstm_in(ctx, x, sc, sh, ng, w_in, b_gate, conv_w, conv_b):
    bsz, seq, d = x.shape
    rows = seq // GRID_W
    assert rows == BLK
    nblk = 1 + GRID_W
    t_tot = nblk * BLK
    n_in = w_in.shape[1]
    n_g = 4 * N_HEADS
    xcol = x.reshape(bsz, rows, GRID_W * d)
    w_gt = w_in[:, 2 * QK_W + 2 * V_W:].T

    def late(width):
        return pl.BlockSpec((None, BLK, width), lambda b, t: (b, jnp.maximum(t - 1, 0), 0))

    def tok(width):
        return pl.BlockSpec((None, BLK, width), lambda b, t: (b, jnp.minimum(t, nblk - 1), 0))

    return pl.pallas_call(
        functools.partial(_mlstm_in_kernel, nblk=nblk),
        grid=(bsz, nblk + 1),
        in_specs=[pl.BlockSpec((None, BLK, d), lambda b, t: (b, 0, 0)),
                  pl.BlockSpec((None, rows, d), lambda b, t: (b, 0, jnp.clip(t - 1, 0, GRID_W - 1))),
                  _sel_spec(), _sel_spec(), _full((1, d)), _full((d, n_in)), _full((n_g, d)),
                  _full((n_g, 1)), _full((3, 2 * QK_W)), _full((1, 2 * QK_W))],
        out_specs=[late(QK_W), late(QK_W), tok(V_W), tok(V_W),
                   pl.BlockSpec((None, n_g, BLK), lambda b, t: (b, 0, jnp.minimum(t, nblk - 1)))],
        out_shape=[jax.ShapeDtypeStruct((bsz, t_tot, QK_W), BF16),
                   jax.ShapeDtypeStruct((bsz, t_tot, QK_W), BF16),
                   jax.ShapeDtypeStruct((bsz, t_tot, V_W), BF16),
                   jax.ShapeDtypeStruct((bsz, t_tot, V_W), BF16),
                   jax.ShapeDtypeStruct((bsz, n_g, t_tot), F32)],
        scratch_shapes=[pltpu.VMEM((2, BLK, 2 * QK_W), F32), pltpu.VMEM((8, 2 * QK_W), F32)],
        compiler_params=_params(("parallel", "arbitrary")),
        name="mlstm_in",
    )(ctx, xcol, sc, sh, ng, w_in, w_gt, b_gate.reshape(n_g, 1), conv_w, conv_b.reshape(1, 2 * QK_W))


def _block_tri(n, sub, lower):
    ri = lax.broadcasted_iota(jnp.int32, (n, n), 0)
    ci = lax.broadcasted_iota(jnp.int32, (n, n), 1)
    shift = sub.bit_length() - 1
    assert sub == 1 << shift
    same = (ri >> shift) == (ci >> shift)
    tri = (ci <= ri) if lower else (ci >= ri)
    return jnp.where(same & tri, 1.0, 0.0).astype(BF16)


def _tri_mask(n, lower):
    ri = lax.broadcasted_iota(jnp.int32, (n, n), 0)
    ci = lax.broadcasted_iota(jnp.int32, (n, n), 1)
    return (ci <= ri) if lower else (ci >= ri)


def _gla_scan_kernel(qf, kf, vf, zf, qb, kb, vb, zb, w2_ref, b2_ref, of_ref, ob_ref, s_ref):
    @pl.when(pl.program_id(1) == 0)
    def _():
        s_ref[...] = jnp.zeros_like(s_ref)

    dirs = ((qf, kf, vf, zf, of_ref), (qb, kb, vb, zb, ob_ref))
    for d, (q_ref, k_ref, v_ref, z_ref, o_ref) in enumerate(dirs):
        fwd = d == 0
        za = z_ref[:, d * GLA_RANK:(d + 1) * GLA_RANK].astype(BF16)
        la = _log_sigmoid(_dot(za, w2_ref[d]) + b2_ref[d]) * (1.0 / GLA_TAU)
        tri = _block_tri(BLK, SUB, lower=fwd)
        hi, lo = _split_bf16(la)
        bcum = _dot(tri, hi) + _dot(tri, lo)
        mask = _tri_mask(SUB, lower=fwd)
        for sub in ((0, 1) if fwd else (1, 0)):
            rs = slice(sub * SUB, (sub + 1) * SUB)
            bs = bcum[rs]
            tot = bs[SUB - 1:SUB] if fwd else bs[0:1]
            mid = 0.5 * tot
            e_mid = jnp.exp(mid)
            e_q = jnp.exp(bs - mid)
            e_k = jnp.exp(mid - bs)
            q32 = q_ref[rs, :].astype(F32)
            k32 = k_ref[rs, :].astype(F32)
            q_mid = q32 * e_q
            q_in = q_mid * e_mid
            k_mid = k32 * e_k
            k_st = k_mid * e_mid
            decay = e_mid * e_mid
            for h in range(N_HEADS):
                ks = slice(h * D_K, (h + 1) * D_K)
                vs = slice(h * D_V, (h + 1) * D_V)
                scores = _dot_nt(q_mid[:, ks].astype(BF16), k_mid[:, ks].astype(BF16))
                p = jnp.where(mask, scores, 0.0).astype(BF16)
                vh = v_ref[rs, vs]
                s_old = s_ref[d, h]
                lhs = jnp.concatenate([p, q_in[:, ks].astype(BF16)], axis=1)
                rhs = jnp.concatenate([vh, s_old.astype(BF16)], axis=0)
                o_ref[rs, vs] = _dot(lhs, rhs).astype(o_ref.dtype)
                k_st_t = k_st[:, ks].T.astype(BF16)
                dcol = jnp.broadcast_to(decay[:, ks], (D_K, D_K)).T
                s_ref[d, h] = s_old * jnp.concatenate([dcol, dcol], axis=1) + _dot(k_st_t, vh)


def _scan_maps(nblk):
    def fwd(b, s):
        return (b, s, 0)

    def bwd(b, s):
        return (b, jnp.where(s == 0, 0, nblk - s), 0)

    return fwd, bwd


def _gla_scan(q, k, v, za, w_a2, b_a2):
    bsz, t_tot, _ = q.shape
    nblk = t_tot // BLK
    fwd, bwd = _scan_maps(nblk)

    def specs(m):
        return [pl.BlockSpec((None, BLK, QK_W), m), pl.BlockSpec((None, BLK, QK_W), m),
                pl.BlockSpec((None, BLK, V_W), m), pl.BlockSpec((None, BLK, 2 * GLA_RANK), m)]

    return pl.pallas_call(
        _gla_scan_kernel,
        grid=(bsz, nblk),
        in_specs=specs(fwd) + specs(bwd) + [_full((2, GLA_RANK, QK_W)), _full((2, 1, QK_W))],
        out_specs=[pl.BlockSpec((None, BLK, V_W), fwd), pl.BlockSpec((None, BLK, V_W), bwd)],
        out_shape=[jax.ShapeDtypeStruct((bsz, t_tot, V_W), BF16)] * 2,
        scratch_shapes=[pltpu.VMEM((2, N_HEADS, D_K, D_V), F32)],
        compiler_params=_params(("parallel", "arbitrary")),
        name="gla_scan",
    )(q, k, v, za, q, k, v, za, w_a2.astype(BF16), b_a2.reshape(2, 1, QK_W))


def _mlstm_scan_kernel(qf, kf, vf, gtf, qb, kb, vb, gtb, of_ref, ob_ref, c_ref, nb_ref, m_ref):
    @pl.when(pl.program_id(1) == 0)
    def _():
        c_ref[...] = jnp.zeros_like(c_ref)
        nb_ref[...] = jnp.zeros_like(nb_ref)
        m_ref[...] = jnp.zeros_like(m_ref)

    ones = jnp.ones((SUB, D_K), BF16)
    dirs = ((qf, kf, vf, gtf, of_ref), (qb, kb, vb, gtb, ob_ref))
    for d, (q_ref, k_ref, v_ref, gt_ref, o_ref) in enumerate(dirs):
        fwd = d == 0
        gt = gt_ref[...]
        gt_hi, gt_lo = _split_bf16(gt)
        tri_r = _block_tri(BLK, SUB, lower=not fwd)
        f_row = _dot(gt_hi, tri_r) + _dot(gt_lo, tri_r)
        mask = _tri_mask(SUB, lower=fwd)
        for sub in ((0, 1) if fwd else (1, 0)):
            rs = slice(sub * SUB, (sub + 1) * SUB)
            for h in range(N_HEADS):
                ks = slice(h * D_K, (h + 1) * D_K)
                vs = slice(h * D_V, (h + 1) * D_V)
                gi = d * N_HEADS + h
                gf_ = 2 * N_HEADS + gi
                fc_r = f_row[gf_:gf_ + 1, rs]
                ic_r = gt[gi:gi + 1, rs]
                fc_b = jnp.broadcast_to(fc_r, (SUB, SUB)).T
                q = q_ref[rs, ks]
                k = k_ref[rs, ks]
                vh = v_ref[rs, vs]
                c_old = c_ref[gi]
                n_old = nb_ref[gi]
                m_old = m_ref[gi]

                d_mat = jnp.where(mask, fc_b + (ic_r - fc_r), -jnp.inf)
                inter = fc_b + m_old
                m_i = jnp.maximum(inter, jnp.max(d_mat, axis=-1, keepdims=True))
                w_inter = jnp.exp(inter - m_i)
                s_mat = _dot_nt(q, k) * jnp.exp(d_mat - m_i)
                lhs = jnp.concatenate([s_mat.astype(BF16), (w_inter * q.astype(F32)).astype(BF16)], axis=1)
                num = _dot(lhs, jnp.concatenate([vh, c_old.astype(BF16)], axis=0))
                den = _dot(lhs, jnp.concatenate([ones, n_old.astype(BF16)], axis=0))
                r = 1.0 / jnp.maximum(jnp.abs(den), jnp.exp(-m_i))
                o_ref[rs, vs] = (num * jnp.concatenate([r, r], axis=1)).astype(o_ref.dtype)

                f_last = fc_r[:, SUB - 1:SUB] if fwd else fc_r[:, 0:1]
                g_r = f_last - fc_r + ic_r
                m_new = jnp.maximum(f_last + m_old, jnp.max(g_r, axis=-1, keepdims=True))
                dec = jnp.exp(f_last + m_old - m_new)[:, 0:1]
                kw_t = (k.astype(F32).T * jnp.exp(g_r - m_new)).astype(BF16)
                c_ref[gi] = dec * c_old + _dot(kw_t, vh)
                nb_ref[gi] = dec * n_old + _dot(kw_t, ones)
                m_ref[gi] = m_new


def _mlstm_scan(q, k, v, gt):
    bsz, t_tot, _ = v.shape
    nblk = t_tot // BLK
    fwd, bwd = _scan_maps(nblk)
    n_g = 4 * N_HEADS

    def specs(m):
        return [pl.BlockSpec((None, BLK, QK_W), m), pl.BlockSpec((None, BLK, QK_W), m),
                pl.BlockSpec((None, BLK, V_W), m),
                pl.BlockSpec((None, n_g, BLK), lambda b, s: (b, 0, m(b, s)[1]))]

    return pl.pallas_call(
        _mlstm_scan_kernel,
        grid=(bsz, nblk),
        in_specs=specs(fwd) + specs(bwd),
        out_specs=[pl.BlockSpec((None, BLK, V_W), fwd), pl.BlockSpec((None, BLK, V_W), bwd)],
        out_shape=[jax.ShapeDtypeStruct((bsz, t_tot, V_W), BF16)] * 2,
        scratch_shapes=[pltpu.VMEM((2 * N_HEADS, D_K, D_V), F32),
                        pltpu.VMEM((2 * N_HEADS, D_K, D_K), F32),
                        pltpu.VMEM((2 * N_HEADS, 1, D_K), F32)],
        compiler_params=_params(("parallel", "arbitrary")),
        name="mlstm_scan",
    )(q, k, v, gt, q, k, v, gt)


def _head_norm_gate(of_ref, ob_ref, gate, ng_ref):
    o = of_ref[...].astype(F32) + ob_ref[...].astype(F32)
    parts = []
    for h in range(N_HEADS):
        oh = o[:, h * D_V:(h + 1) * D_V]
        ms = jnp.mean(oh * oh, axis=-1, keepdims=True)
        parts.append(oh * lax.rsqrt(ms + EPS))
    return (jnp.concatenate(parts, axis=1) * ng_ref[...] * gate).astype(BF16)


def _gla_out_kernel(of_ref, ob_ref, r_ref, ctx_ref, x_ref, g1_ref, ng_ref, w_ref,
                    ctx_out, x_out):
    r = r_ref[...].astype(F32)
    y = _dot(_head_norm_gate(of_ref, ob_ref, r * _sigmoid(r), ng_ref), w_ref[...])
    t = pl.program_id(1)

    @pl.when(t == 0)
    def _():
        ctx_out[...] = ctx_ref[...] + g1_ref[...] * y

    @pl.when(t != 0)
    def _():
        x_out[...] = x_ref[...] + g1_ref[...] * y


def _gla_out(o_f, o_b, r, ctx, x, g1, ng, w_out):
    bsz, seq, d = x.shape
    nblk = 1 + seq // BLK
    tok = pl.BlockSpec((None, BLK, V_W), lambda b, t: (b, t, 0))
    ctx_spec = pl.BlockSpec((None, BLK, d), lambda b, t: (b, 0, 0))
    x_spec = pl.BlockSpec((None, BLK, d), lambda b, t: (b, jnp.maximum(t - 1, 0), 0))
    return pl.pallas_call(
        _gla_out_kernel,
        grid=(bsz, nblk),
        in_specs=[tok, tok, tok, ctx_spec, x_spec, _sel_spec(), _full((1, V_W)), _full((V_W, d))],
        out_specs=[ctx_spec, x_spec],
        out_shape=[jax.ShapeDtypeStruct(ctx.shape, F32), jax.ShapeDtypeStruct(x.shape, F32)],
        compiler_params=_params(("parallel", "arbitrary")),
        name="gla_out",
    )(o_f, o_b, r, ctx, x, g1, ng, w_out)


def _mlstm_out_kernel(of_ref, ob_ref, og_ref, x_ref, g1_ref, ng_ref, w_ref, x_out):
    y = _dot(_head_norm_gate(of_ref, ob_ref, _sigmoid(og_ref[...].astype(F32)), ng_ref), w_ref[...])
    x_out[...] = x_ref[...] + g1_ref[...] * y


def _mlstm_out(h_f, h_b, og, x, g1, ng, w_out):
    bsz, seq, d = x.shape
    rows = seq // GRID_W
    xcol = x.reshape(bsz, rows, GRID_W * d)
    tok = pl.BlockSpec((None, BLK, V_W), lambda b, c: (b, c + 1, 0))
    col = pl.BlockSpec((None, rows, d), lambda b, c: (b, 0, c))
    out = pl.pallas_call(
        _mlstm_out_kernel,
        grid=(bsz, GRID_W),
        in_specs=[tok, tok, tok, col,
                  pl.BlockSpec((None, 1, d), lambda b, c: (b, 0, 0)),
                  _full((1, V_W)), _full((V_W, d))],
        out_specs=col,
        out_shape=jax.ShapeDtypeStruct(xcol.shape, F32),
        compiler_params=_params(("parallel", "arbitrary")),
        name="mlstm_out",
    )(h_f, h_b, og, xcol, g1, ng, w_out)
    return out.reshape(bsz, seq, d)


GELU_K0 = 0.7978845608028654
GELU_K1 = GELU_K0 * 0.044715


def _ffn_kernel(*refs, width, tm, vertical, final_norm):
    refs = list(refs)
    x_ref = refs.pop(0)
    xu_ref, xd_ref = (refs.pop(0), refs.pop(0)) if vertical else (None, None)
    sc_ref, sh_ref, g2_ref, ng_ref, wa_ref, wg_ref, cw_ref, cb_ref, wd_ref = refs[:9]
    refs = refs[9:]
    fg_ref = refs.pop(0) if final_norm else None
    out_ref, hext_ref, acc_ref = refs

    t = pl.program_id(1)
    nt = pl.num_programs(1)
    halo = width if vertical else 0
    ext = tm + 2 * halo

    def hmod(xt):
        return _rms_mod(xt, ng_ref[...], sc_ref[...], sh_ref[...]).astype(BF16)

    x = x_ref[...]
    hext_ref[halo:halo + tm, :] = hmod(x)
    if vertical:
        zeros = jnp.zeros((halo, x.shape[1]), BF16)
        hext_ref[0:halo, :] = jnp.where(t == 0, zeros, hmod(xu_ref[...]))
        hext_ref[halo + tm:ext, :] = jnp.where(t == nt - 1, zeros, hmod(xd_ref[...]))

    assert width & (width - 1) == 0
    wpos = lax.broadcasted_iota(jnp.int32, (tm, FF_CHUNK), 0) & (width - 1)
    first_col = wpos == 0
    last_col = wpos == width - 1
    taps = (0, 1, 2) if vertical else (1,)

    def up(c):
        a = _dot(hext_ref[halo:halo + tm, :], wa_ref[c])
        g = _dot(hext_ref[...], wg_ref[c])
        return a, g

    def activation(c, a, g):
        cw = cw_ref[c]
        u = []
        for dw in range(3):
            s = None
            for dr in taps:
                term = g[dr * halo:dr * halo + tm] * cw[3 * dr + dw:3 * dr + dw + 1]
                s = term if s is None else s + term
            u.append(s)
        conv = ((u[1] + cb_ref[c]) + jnp.where(first_col, 0.0, pltpu.roll(u[0], 1, 0))
                + jnp.where(last_col, 0.0, pltpu.roll(u[2], tm - 1, 0)))
        th = jnp.tanh(conv * (GELU_K0 + GELU_K1 * (conv * conv)))
        return ((conv + conv * th) * a).astype(BF16)

    nxt = up(0)
    for c in range(N_FF_CHUNKS):
        a, g = nxt
        if c + 1 < N_FF_CHUNKS:
            nxt = up(c + 1)
        p = _dot(activation(c, a, g), wd_ref[c])
        acc_ref[...] = p if c == 0 else acc_ref[...] + p
    y = x + g2_ref[...] * acc_ref[...]
    if final_norm:
        ms = jnp.mean(y * y, axis=-1, keepdims=True)
        y = y * lax.rsqrt(ms + EPS) * fg_ref[...]
    out_ref[...] = y


def _conv_ffn(x, sc, sh, g2, ng, wa, wg, cw, cb, wd, *, width, tm, vertical, final_g=None):
    bsz, seq, d = x.shape
    nt = seq // tm
    vec = pl.BlockSpec((None, 1, d), lambda b, t: (b, 0, 0))
    tile = pl.BlockSpec((None, tm, d), lambda b, t: (b, t, 0))
    in_specs = [tile]
    args = [x]
    if vertical:
        rpt = tm // width
        nrows = seq // width
        in_specs += [pl.BlockSpec((None, width, d), lambda b, t: (b, jnp.maximum(t * rpt - 1, 0), 0)),
                     pl.BlockSpec((None, width, d), lambda b, t: (b, jnp.minimum((t + 1) * rpt, nrows - 1), 0))]
        args += [x, x]
    in_specs += [vec, vec, vec, _full((1, d)), _full(wa.shape), _full(wg.shape), _full(cw.shape),
                 _full(cb.shape), _full(wd.shape)]
    args += [sc, sh, g2, ng, wa, wg, cw, cb, wd]
    if final_g is not None:
        in_specs.append(_full((1, d)))
        args.append(final_g)
    ext = tm + (2 * width if vertical else 0)
    return pl.pallas_call(
        functools.partial(_ffn_kernel, width=width, tm=tm, vertical=vertical,
                          final_norm=final_g is not None),
        grid=(bsz, nt),
        in_specs=in_specs,
        out_specs=tile,
        out_shape=jax.ShapeDtypeStruct(x.shape, F32),
        scratch_shapes=[pltpu.VMEM((ext, d), BF16), pltpu.VMEM((tm, d), F32)],
        compiler_params=_params(("parallel", "arbitrary")),
        name="conv_ffn_grid" if vertical else "conv_ffn_seq",
    )(*args)


def _ffn_weights(w_up, conv_w, conv_b, w_down):
    d = w_up.shape[0]

    def chunked(w):
        return w.reshape(d, N_FF_CHUNKS, FF_CHUNK).transpose(1, 0, 2).astype(BF16)

    wa = chunked(w_up[:, :D_FF])
    wg = chunked(w_up[:, D_FF:])
    cw = conv_w.reshape(9, N_FF_CHUNKS, FF_CHUNK).transpose(1, 0, 2)
    cb = conv_b.reshape(N_FF_CHUNKS, 1, FF_CHUNK)
    wd = (0.5 * w_down).reshape(N_FF_CHUNKS, FF_CHUNK, d).astype(BF16)
    return wa, wg, cw, cb, wd


def kernel(x, c, ctx, c_ctx, ada_w, ada_b, norm_mix_g, norm_ffn_g, gla_w_in, gla_w_a2, gla_b_a2, gla_norm_g, gla_w_out, mlstm_w_in, mlstm_b_gate, mlstm_conv_w, mlstm_conv_b, mlstm_norm_g, mlstm_w_out, ffn_w_up, ffn_conv_w, ffn_conv_b, ffn_w_down, final_norm_g):
    bsz, seq, d = x.shape
    depth = ada_w.shape[0]
    assert bsz == 2 and d == D_MODEL and ctx.shape[1] == BLK and seq == BLK * GRID_W and depth == 2

    cvec = jnp.zeros((8, d), F32).at[:bsz].set(c).at[bsz].set(c_ctx)
    mods = _modulation(cvec, ada_w, ada_b)

    def mod(i, j):
        return mods[i, :3, j * d:(j + 1) * d].reshape(3, 1, d)

    def row(v):
        return v.reshape(1, -1)

    sh1, sc1, g1, sh2, sc2, g2 = (mod(0, j) for j in range(N_MOD))
    q, k, v, r, za = _gla_in(ctx, x, sc1, sh1, row(norm_mix_g[0]), gla_w_in[0].astype(BF16))
    o_f, o_b = _gla_scan(q, k, v, za, gla_w_a2[0], gla_b_a2[0])
    ctx, x = _gla_out(o_f, o_b, r, ctx, x, g1, row(gla_norm_g[0]), gla_w_out[0].astype(BF16))
    ffn_w = _ffn_weights(ffn_w_up[0], ffn_conv_w[0], ffn_conv_b[0], ffn_w_down[0])
    x = _conv_ffn(x, sc2[:bsz], sh2[:bsz], g2[:bsz], row(norm_ffn_g[0]), *ffn_w,
                  width=GRID_W, tm=FFN_ROWS * GRID_W, vertical=True)
    ctx = _conv_ffn(ctx, jnp.broadcast_to(sc2[2:], (bsz, 1, d)), jnp.broadcast_to(sh2[2:], (bsz, 1, d)),
                    jnp.broadcast_to(g2[2:], (bsz, 1, d)), row(norm_ffn_g[0]), *ffn_w,
                    width=BLK, tm=BLK, vertical=False)

    sh1, sc1, g1, sh2, sc2, g2 = (mod(1, j) for j in range(N_MOD))
    q, k, v, og, gt = _mlstm_in(ctx, x, sc1, sh1, row(norm_mix_g[1]), mlstm_w_in[0].astype(BF16),
                                mlstm_b_gate[0], mlstm_conv_w[0], mlstm_conv_b[0])
    h_f, h_b = _mlstm_scan(q, k, v, gt)
    x = _mlstm_out(h_f, h_b, og, x, g1[:bsz], row(mlstm_norm_g[0]), mlstm_w_out[0].astype(BF16))
    ffn_w = _ffn_weights(ffn_w_up[1], ffn_conv_w[1], ffn_conv_b[1], ffn_w_down[1])
    return _conv_ffn(x, sc2[:bsz], sh2[:bsz], g2[:bsz], row(norm_ffn_g[1]), *ffn_w,
                     width=GRID_W, tm=FFN_ROWS * GRID_W, vertical=True, final_g=row(final_norm_g))
```

```python
import functools

import jax
import jax.numpy as jnp
from jax import lax
from jax.experimental import pallas as pl
from jax.experimental.pallas import tpu as pltpu

D_MODEL = 1024
GRID_W = 64
N_HEADS = 4
D_K = 128
D_V = 256
QK_W = N_HEADS * D_K
V_W = N_HEADS * D_V
GLA_RANK = 16
GLA_TAU = 16.0
D_FF = 2816
N_MOD = 6
EPS = 1e-6

BLK = 256
SUB = 128
FF_CHUNK = 256
N_FF_CHUNKS = D_FF // FF_CHUNK
FFN_ROWS = 16
FFN_DOWN_GROUP = 2
FFN_PIECE_ROWS = 256
VMEM_LIMIT_BYTES = 56 * 1024 * 1024

F32 = jnp.float32
BF16 = jnp.bfloat16


def _dot(a, b):
    return jnp.dot(a, b, preferred_element_type=F32)


def _dot_nt(a, b):
    return lax.dot_general(a, b, (((1,), (1,)), ((), ())), preferred_element_type=F32)


def _sigmoid(x):
    return 1.0 / (1.0 + jnp.exp(-x))


def _log_sigmoid(x):
    return jnp.minimum(x, 0.0) - jnp.log(1.0 + jnp.exp(-jnp.abs(x)))


def _rms_mod(x, ng, sc, sh):
    ms = jnp.mean(x * x, axis=-1, keepdims=True)
    return (x * lax.rsqrt(ms + EPS) * ng) * (1.0 + sc) + sh


def _split_bf16(x):
    hi = x.astype(BF16)
    lo = (x - hi.astype(F32)).astype(BF16)
    return hi, lo


def _params(sem):
    return pltpu.CompilerParams(dimension_semantics=sem, vmem_limit_bytes=VMEM_LIMIT_BYTES)


def _full(shape):
    n = len(shape)
    return pl.BlockSpec(shape, lambda *_: (0,) * n)


def _resident(shape):
    n = len(shape)
    return pl.BlockSpec(shape, lambda *_: (0,) * n, pipeline_mode=pl.Buffered(1))


def _sel_spec():
    return pl.BlockSpec((None, 1, D_MODEL), lambda b, t: (jnp.where(t == 0, 2, b), 0, 0))


def _mod_kernel(c_ref, w_ref, b_ref, o_ref):
    c = c_ref[...]
    s = c * _sigmoid(c)
    o_ref[...] = jnp.dot(s, w_ref[...], preferred_element_type=F32,
                         precision=lax.Precision.HIGHEST) + b_ref[...]


def _modulation(cvec, ada_w, ada_b):
    depth = ada_w.shape[0]
    n = N_MOD * D_MODEL
    tn = 1536
    return pl.pallas_call(
        _mod_kernel,
        grid=(depth, n // tn),
        in_specs=[pl.BlockSpec((8, D_MODEL), lambda i, j: (0, 0)),
                  pl.BlockSpec((None, D_MODEL, tn), lambda i, j: (i, 0, j)),
                  pl.BlockSpec((None, 1, tn), lambda i, j: (i, 0, j))],
        out_specs=pl.BlockSpec((None, 8, tn), lambda i, j: (i, 0, j)),
        out_shape=jax.ShapeDtypeStruct((depth, 8, n), F32),
        compiler_params=_params(("parallel", "parallel")),
        name="modulation",
    )(cvec, ada_w, ada_b.reshape(depth, 1, n))


def _load_tile(ctx_ref, x_ref):
    t = pl.program_id(1)
    return jnp.where(t == 0, ctx_ref[...], x_ref[...])


def _gla_in_kernel(ctx_ref, x_ref, sc_ref, sh_ref, ng_ref, w_ref,
                   q_ref, k_ref, v_ref, r_ref, za_ref):
    h = _rms_mod(_load_tile(ctx_ref, x_ref), ng_ref[...], sc_ref[...], sh_ref[...]).astype(BF16)
    q_ref[...] = (_dot(h, w_ref[:, 0:QK_W]) * D_K ** -0.5).astype(BF16)
    k_ref[...] = _dot(h, w_ref[:, QK_W:2 * QK_W]).astype(BF16)
    v_ref[...] = _dot(h, w_ref[:, 2 * QK_W:2 * QK_W + V_W]).astype(BF16)
    r_ref[...] = _dot(h, w_ref[:, 2 * QK_W + V_W:2 * QK_W + 2 * V_W]).astype(BF16)
    za_ref[...] = _dot(h, w_ref[:, 2 * QK_W + 2 * V_W:])


def _gla_in(ctx, x, sc, sh, ng, w_in):
    bsz, seq, d = x.shape
    nblk = 1 + seq // BLK
    t_tot = nblk * BLK
    n_in = w_in.shape[1]

    def tok(width):
        return pl.BlockSpec((None, BLK, width), lambda b, t: (b, t, 0))

    return pl.pallas_call(
        _gla_in_kernel,
        grid=(bsz, nblk),
        in_specs=[pl.BlockSpec((None, BLK, d), lambda b, t: (b, 0, 0)),
                  pl.BlockSpec((None, BLK, d), lambda b, t: (b, jnp.maximum(t - 1, 0), 0)),
                  _sel_spec(), _sel_spec(), _full((1, d)), _full((d, n_in))],
        out_specs=[tok(QK_W), tok(QK_W), tok(V_W), tok(V_W), tok(2 * GLA_RANK)],
        out_shape=[jax.ShapeDtypeStruct((bsz, t_tot, QK_W), BF16),
                   jax.ShapeDtypeStruct((bsz, t_tot, QK_W), BF16),
                   jax.ShapeDtypeStruct((bsz, t_tot, V_W), BF16),
                   jax.ShapeDtypeStruct((bsz, t_tot, V_W), BF16),
                   jax.ShapeDtypeStruct((bsz, t_tot, 2 * GLA_RANK), F32)],
        compiler_params=_params(("parallel", "arbitrary")),
        name="gla_in",
    )(ctx, x, sc, sh, ng, w_in)


def _mlstm_in_kernel(ctx_ref, x_ref, sc_ref, sh_ref, ng_ref, w_ref, wgt_ref, bgt_ref, cw_ref, cb_ref,
                     q_ref, k_ref, v_ref, og_ref, gt_ref, pre_ref, last_ref, *, nblk):
    t = pl.program_id(1)

    @pl.when(t == 0)
    def _():
        last_ref[...] = jnp.zeros_like(last_ref)

    @pl.when(t < nblk)
    def _():
        h = _rms_mod(_load_tile(ctx_ref, x_ref), ng_ref[...], sc_ref[...], sh_ref[...]).astype(BF16)
        pre_ref[t % 2] = _dot(h, w_ref[:, 0:2 * QK_W])
        v_ref[...] = _dot(h, w_ref[:, 2 * QK_W:2 * QK_W + V_W]).astype(BF16)
        og_ref[...] = _dot(h, w_ref[:, 2 * QK_W + V_W:2 * QK_W + 2 * V_W]).astype(BF16)
        gt = _dot_nt(wgt_ref[...], h) + bgt_ref[...]
        is_forget = lax.broadcasted_iota(jnp.int32, gt.shape, 0) >= 2 * N_HEADS
        gt_ref[...] = jnp.where(is_forget, _log_sigmoid(gt), gt)

    @pl.when(t > 0)
    def _():
        blk = t - 1
        x = pre_ref[blk % 2]
        has_prev = jnp.logical_and(blk != 0, blk != 1)
        has_next = jnp.logical_and(blk != 0, blk != nblk - 1)
        prev = jnp.where(has_prev, last_ref[0:1, :], 0.0)
        nxt = jnp.where(has_next, pre_ref[t % 2, 0:1, :], 0.0)
        row = lax.broadcasted_iota(jnp.int32, (BLK, 1), 0)
        x_m1 = jnp.where(row == 0, prev, pltpu.roll(x, 1, 0))
        x_p1 = jnp.where(row == BLK - 1, nxt, pltpu.roll(x, BLK - 1, 0))
        qk = x_m1 * cw_ref[0:1, :] + x * cw_ref[1:2, :] + x_p1 * cw_ref[2:3, :] + cb_ref[...]
        qk = qk * _sigmoid(qk)
        q_ref[...] = qk[:, :QK_W].astype(BF16)
        k_ref[...] = (qk[:, QK_W:] * D_K ** -0.5).astype(BF16)
        last_ref[0:1, :] = x[BLK - 1:BLK, :]


def _mlstm_in(ctx, x, sc, sh, ng, w_in, b_gate, conv_w, conv_b):
    bsz, seq, d = x.shape
    rows = seq // GRID_W
    assert rows == BLK
    nblk = 1 + GRID_W
    t_tot = nblk * BLK
    n_in = w_in.shape[1]
    n_g = 4 * N_HEADS
    xcol = x.reshape(bsz, rows, GRID_W * d)
    w_gt = w_in[:, 2 * QK_W + 2 * V_W:].T

    def late(width):
        return pl.BlockSpec((None, BLK, width), lambda b, t: (b, jnp.maximum(t - 1, 0), 0))

    def tok(width):
        return pl.BlockSpec((None, BLK, width), lambda b, t: (b, jnp.minimum(t, nblk - 1), 0))

    return pl.pallas_call(
        functools.partial(_mlstm_in_kernel, nblk=nblk),
        grid=(bsz, nblk + 1),
        in_specs=[pl.BlockSpec((None, BLK, d), lambda b, t: (b, 0, 0)),
                  pl.BlockSpec((None, rows, d), lambda b, t: (b, 0, jnp.clip(t - 1, 0, GRID_W - 1))),
                  _sel_spec(), _sel_spec(), _full((1, d)), _full((d, n_in)), _full((n_g, d)),
                  _full((n_g, 1)), _full((3, 2 * QK_W)), _full((1, 2 * QK_W))],
        out_specs=[late(QK_W), late(QK_W), tok(V_W), tok(V_W),
                   pl.BlockSpec((None, n_g, BLK), lambda b, t: (b, 0, jnp.minimum(t, nblk - 1)))],
        out_shape=[jax.ShapeDtypeStruct((bsz, t_tot, QK_W), BF16),
                   jax.ShapeDtypeStruct((bsz, t_tot, QK_W), BF16),
                   jax.ShapeDtypeStruct((bsz, t_tot, V_W), BF16),
                   jax.ShapeDtypeStruct((bsz, t_tot, V_W), BF16),
                   jax.ShapeDtypeStruct((bsz, n_g, t_tot), F32)],
        scratch_shapes=[pltpu.VMEM((2, BLK, 2 * QK_W), F32), pltpu.VMEM((8, 2 * QK_W), F32)],
        compiler_params=_params(("parallel", "arbitrary")),
        name="mlstm_in",
    )(ctx, xcol, sc, sh, ng, w_in, w_gt, b_gate.reshape(n_g, 1), conv_w, conv_b.reshape(1, 2 * QK_W))


def _block_tri(n, sub, lower):
    ri = lax.broadcasted_iota(jnp.int32, (n, n), 0)
    ci = lax.broadcasted_iota(jnp.int32, (n, n), 1)
    shift = sub.bit_length() - 1
    assert sub == 1 << shift
    same = (ri >> shift) == (ci >> shift)
    tri = (ci <= ri) if lower else (ci >= ri)
    return jnp.where(same & tri, 1.0, 0.0).astype(BF16)


def _tri_mask(n, lower):
    ri = lax.broadcasted_iota(jnp.int32, (n, n), 0)
    ci = lax.broadcasted_iota(jnp.int32, (n, n), 1)
    return (ci <= ri) if lower else (ci >= ri)


def _staggered_round_robin(groups, lag):
    live, rnd = [], 0
    while live or rnd <= lag * (len(groups) - 1):
        if rnd % lag == 0 and rnd // lag < len(groups):
            live = live + list(groups[rnd // lag])
        alive = []
        for gen in live:
            try:
                next(gen)
                alive.append(gen)
            except StopIteration:
                pass
        live, rnd = alive, rnd + 1


def _gla_scan_kernel(qf, kf, vf, zf, qb, kb, vb, zb, w2_ref, b2_ref, of_ref, ob_ref, *s_refs):
    @pl.when(pl.program_id(1) == 0)
    def _():
        for s_ref in s_refs:
            s_ref[...] = jnp.zeros_like(s_ref)

    dirs = ((qf, kf, vf, zf, of_ref), (qb, kb, vb, zb, ob_ref))
    tris = (_block_tri(BLK, SUB, lower=True), _block_tri(BLK, SUB, lower=False))
    masks = (_tri_mask(SUB, lower=True), _tri_mask(SUB, lower=False))

    def unit(d, h):
        q_ref, k_ref, v_ref, z_ref, o_ref = dirs[d]
        s_ref = s_refs[d * N_HEADS + h]
        fwd = d == 0
        ks = slice(h * D_K, (h + 1) * D_K)
        vs = slice(h * D_V, (h + 1) * D_V)
        za = z_ref[:, d * GLA_RANK:(d + 1) * GLA_RANK].astype(BF16)
        y = _dot(za, w2_ref[d, :, ks])
        yield
        la = _log_sigmoid(y + b2_ref[d, :, ks]) * (1.0 / GLA_TAU)
        hi, lo = _split_bf16(la)
        bcum = _dot(tris[d], jnp.concatenate([hi, lo], axis=1))
        yield
        bcum = bcum[:, :D_K] + bcum[:, D_K:]
        ops = {}
        for sub in (0, 1):
            rs = slice(sub * SUB, (sub + 1) * SUB)
            bs = bcum[rs]
            tot = bs[SUB - 1:SUB] if fwd else bs[0:1]
            mid = 0.5 * tot
            e_mid = jnp.exp(mid)
            q_mid = q_ref[rs, ks].astype(F32) * jnp.exp(bs - mid)
            k_mid = k_ref[rs, ks].astype(F32) * jnp.exp(mid - bs)
            q_in = (q_mid * e_mid).astype(BF16)
            k_st_t = (k_mid * e_mid).T.astype(BF16)
            ops[sub] = (q_mid.astype(BF16), q_in, k_mid.astype(BF16), k_st_t, e_mid * e_mid)
            yield
        for sub in ((0, 1) if fwd else (1, 0)):
            rs = slice(sub * SUB, (sub + 1) * SUB)
            q_mid, q_in, k_mid, k_st_t, decay = ops[sub]
            scores = _dot_nt(q_mid, k_mid)
            yield
            p = jnp.where(masks[d], scores, 0.0).astype(BF16)
            vh = v_ref[rs, vs]
            s_old = s_ref[...]
            o = _dot(jnp.concatenate([p, q_in], axis=1), jnp.concatenate([vh, s_old.astype(BF16)], axis=0))
            upd = _dot(k_st_t, vh)
            yield
            o_ref[rs, vs] = o.astype(o_ref.dtype)
            dcol = jnp.broadcast_to(decay, (D_K, D_K)).T
            s_ref[...] = s_old * jnp.concatenate([dcol, dcol], axis=1) + upd
            yield

    _staggered_round_robin([[unit(d, h) for h in hs for d in (0, 1)] for hs in ((0, 1), (2, 3))], lag=4)


def _scan_maps(nblk):
    def fwd(b, s):
        return (b, s, 0)

    def bwd(b, s):
        return (b, jnp.where(s == 0, 0, nblk - s), 0)

    return fwd, bwd


def _gla_scan(q, k, v, za, w_a2, b_a2):
    bsz, t_tot, _ = q.shape
    nblk = t_tot // BLK
    fwd, bwd = _scan_maps(nblk)

    def specs(m):
        return [pl.BlockSpec((None, BLK, QK_W), m), pl.BlockSpec((None, BLK, QK_W), m),
                pl.BlockSpec((None, BLK, V_W), m), pl.BlockSpec((None, BLK, 2 * GLA_RANK), m)]

    return pl.pallas_call(
        _gla_scan_kernel,
        grid=(bsz, nblk),
        in_specs=specs(fwd) + specs(bwd) + [_full((2, GLA_RANK, QK_W)), _full((2, 1, QK_W))],
        out_specs=[pl.BlockSpec((None, BLK, V_W), fwd), pl.BlockSpec((None, BLK, V_W), bwd)],
        out_shape=[jax.ShapeDtypeStruct((bsz, t_tot, V_W), BF16)] * 2,
        scratch_shapes=[pltpu.VMEM((D_K, D_V), F32)] * (2 * N_HEADS),
        compiler_params=_params(("parallel", "arbitrary")),
        name="gla_scan",
    )(q, k, v, za, q, k, v, za, w_a2.astype(BF16), b_a2.reshape(2, 1, QK_W))


def _mlstm_scan_kernel(qf, kf, vf, gtf, qb, kb, vb, gtb, of_ref, ob_ref, *state_refs):
    n_units = 2 * N_HEADS
    c_refs, nb_refs, m_refs = (state_refs[i * n_units:(i + 1) * n_units] for i in range(3))

    @pl.when(pl.program_id(1) == 0)
    def _():
        for ref in state_refs:
            ref[...] = jnp.zeros_like(ref)

    ones = jnp.ones((SUB, D_K), BF16)
    dirs = ((qf, kf, vf, gtf, of_ref), (qb, kb, vb, gtb, ob_ref))
    gates = []
    for d, (_, _, _, gt_ref, _) in enumerate(dirs):
        gt = gt_ref[...]
        gt_hi, gt_lo = _split_bf16(gt)
        tri_r = _block_tri(BLK, SUB, lower=d != 0)
        gates.append((gt, _dot(gt_hi, tri_r) + _dot(gt_lo, tri_r)))
    masks = (_tri_mask(SUB, lower=True), _tri_mask(SUB, lower=False))
    def unit(d, h):
        q_ref, k_ref, v_ref, _, o_ref = dirs[d]
        fwd = d == 0
        gt, f_row = gates[d]
        ks = slice(h * D_K, (h + 1) * D_K)
        vs = slice(h * D_V, (h + 1) * D_V)
        gi = d * N_HEADS + h
        gf_ = 2 * N_HEADS + gi
        c_ref, nb_ref, m_ref = c_refs[gi], nb_refs[gi], m_refs[gi]
        pre = {}
        for sub in (0, 1):
            rs = slice(sub * SUB, (sub + 1) * SUB)
            fc_r = f_row[gf_:gf_ + 1, rs]
            ic_r = gt[gi:gi + 1, rs]
            fc_b = jnp.broadcast_to(fc_r, (SUB, SUB)).T
            d_mat = jnp.where(masks[d], fc_b + (ic_r - fc_r), -jnp.inf)
            d_max = jnp.max(d_mat, axis=-1, keepdims=True)
            qk = _dot_nt(q_ref[rs, ks], k_ref[rs, ks])
            k_t = k_ref[rs, ks].astype(F32).T
            pre[sub] = (fc_r, ic_r, fc_b, d_mat, d_max, qk, k_t)
            yield
        for sub in ((0, 1) if fwd else (1, 0)):
            rs = slice(sub * SUB, (sub + 1) * SUB)
            fc_r, ic_r, fc_b, d_mat, d_max, qk, k_t = pre[sub]
            vh = v_ref[rs, vs]
            c_old = c_ref[...]
            n_old = nb_ref[...]
            m_old = m_ref[...]
            inter = fc_b + m_old
            m_i = jnp.maximum(inter, d_max)
            w_inter = jnp.exp(inter - m_i)
            s_mat = qk * jnp.exp(d_mat - m_i)
            lhs = jnp.concatenate([s_mat.astype(BF16),
                                   (w_inter * q_ref[rs, ks].astype(F32)).astype(BF16)], axis=1)
            num = _dot(lhs, jnp.concatenate([vh, c_old.astype(BF16)], axis=0))
            den = _dot(lhs, jnp.concatenate([ones, n_old.astype(BF16)], axis=0))
            f_last = fc_r[:, SUB - 1:SUB] if fwd else fc_r[:, 0:1]
            g_r = f_last - fc_r + ic_r
            m_new = jnp.maximum(f_last + m_old, jnp.max(g_r, axis=-1, keepdims=True))
            dec = jnp.exp(f_last + m_old - m_new)[:, 0:1]
            kw_t = (k_t * jnp.exp(g_r - m_new)).astype(BF16)
            c_upd = _dot(kw_t, vh)
            n_upd = _dot(kw_t, ones)
            yield
            r = 1.0 / jnp.maximum(jnp.abs(den), jnp.exp(-m_i))
            o_ref[rs, vs] = (num * jnp.concatenate([r, r], axis=1)).astype(o_ref.dtype)
            c_ref[...] = dec * c_old + c_upd
            nb_ref[...] = dec * n_old + n_upd
            m_ref[...] = m_new
            yield

    _staggered_round_robin([[unit(d, h) for h in hs for d in (0, 1)] for hs in ((0, 1), (2, 3))], lag=2)


def _mlstm_scan(q, k, v, gt):
    bsz, t_tot, _ = v.shape
    nblk = t_tot // BLK
    fwd, bwd = _scan_maps(nblk)
    n_g = 4 * N_HEADS

    def specs(m):
        return [pl.BlockSpec((None, BLK, QK_W), m), pl.BlockSpec((None, BLK, QK_W), m),
                pl.BlockSpec((None, BLK, V_W), m),
                pl.BlockSpec((None, n_g, BLK), lambda b, s: (b, 0, m(b, s)[1]))]

    return pl.pallas_call(
        _mlstm_scan_kernel,
        grid=(bsz, nblk),
        in_specs=specs(fwd) + specs(bwd),
        out_specs=[pl.BlockSpec((None, BLK, V_W), fwd), pl.BlockSpec((None, BLK, V_W), bwd)],
        out_shape=[jax.ShapeDtypeStruct((bsz, t_tot, V_W), BF16)] * 2,
        scratch_shapes=([pltpu.VMEM((D_K, D_V), F32)] * (2 * N_HEADS)
                        + [pltpu.VMEM((D_K, D_K), F32)] * (2 * N_HEADS)
                        + [pltpu.VMEM((1, D_K), F32)] * (2 * N_HEADS)),
        compiler_params=_params(("parallel", "arbitrary")),
        name="mlstm_scan",
    )(q, k, v, gt, q, k, v, gt)


def _head_norm_gate(of_ref, ob_ref, gate, ng_ref):
    o = of_ref[...].astype(F32) + ob_ref[...].astype(F32)
    parts = []
    for h in range(N_HEADS):
        oh = o[:, h * D_V:(h + 1) * D_V]
        ms = jnp.mean(oh * oh, axis=-1, keepdims=True)
        parts.append(oh * lax.rsqrt(ms + EPS))
    return (jnp.concatenate(parts, axis=1) * ng_ref[...] * gate).astype(BF16)


def _gla_out_kernel(of_ref, ob_ref, r_ref, ctx_ref, x_ref, g1_ref, ng_ref, w_ref,
                    ctx_out, x_out):
    r = r_ref[...].astype(F32)
    y = _dot(_head_norm_gate(of_ref, ob_ref, r * _sigmoid(r), ng_ref), w_ref[...])
    t = pl.program_id(1)

    @pl.when(t == 0)
    def _():
        ctx_out[...] = ctx_ref[...] + g1_ref[...] * y

    @pl.when(t != 0)
    def _():
        x_out[...] = x_ref[...] + g1_ref[...] * y


def _gla_out(o_f, o_b, r, ctx, x, g1, ng, w_out):
    bsz, seq, d = x.shape
    nblk = 1 + seq // BLK
    tok = pl.BlockSpec((None, BLK, V_W), lambda b, t: (b, t, 0))
    ctx_spec = pl.BlockSpec((None, BLK, d), lambda b, t: (b, 0, 0))
    x_spec = pl.BlockSpec((None, BLK, d), lambda b, t: (b, jnp.maximum(t - 1, 0), 0))
    return pl.pallas_call(
        _gla_out_kernel,
        grid=(bsz, nblk),
        in_specs=[tok, tok, tok, ctx_spec, x_spec, _sel_spec(), _full((1, V_W)), _full((V_W, d))],
        out_specs=[ctx_spec, x_spec],
        out_shape=[jax.ShapeDtypeStruct(ctx.shape, F32), jax.ShapeDtypeStruct(x.shape, F32)],
        compiler_params=_params(("parallel", "arbitrary")),
        name="gla_out",
    )(o_f, o_b, r, ctx, x, g1, ng, w_out)


def _mlstm_out_kernel(of_ref, ob_ref, og_ref, x_ref, g1_ref, ng_ref, w_ref, x_out):
    y = _dot(_head_norm_gate(of_ref, ob_ref, _sigmoid(og_ref[...].astype(F32)), ng_ref), w_ref[...])
    x_out[...] = x_ref[...] + g1_ref[...] * y


def _mlstm_out(h_f, h_b, og, x, g1, ng, w_out):
    bsz, seq, d = x.shape
    rows = seq // GRID_W
    xcol = x.reshape(bsz, rows, GRID_W * d)
    tok = pl.BlockSpec((None, BLK, V_W), lambda b, c: (b, c + 1, 0))
    col = pl.BlockSpec((None, rows, d), lambda b, c: (b, 0, c))
    out = pl.pallas_call(
        _mlstm_out_kernel,
        grid=(bsz, GRID_W),
        in_specs=[tok, tok, tok, col,
                  pl.BlockSpec((None, 1, d), lambda b, c: (b, 0, 0)),
                  _full((1, V_W)), _full((V_W, d))],
        out_specs=col,
        out_shape=jax.ShapeDtypeStruct(xcol.shape, F32),
        compiler_params=_params(("parallel", "arbitrary")),
        name="mlstm_out",
    )(h_f, h_b, og, xcol, g1, ng, w_out)
    return out.reshape(bsz, seq, d)


GELU_K0 = 0.7978845608028654
GELU_K1 = GELU_K0 * 0.044715


def _ffn_kernel(*refs, width, tm, vertical, final_norm):
    refs = list(refs)
    x_ref = refs.pop(0)
    xu_ref, xd_ref = (refs.pop(0), refs.pop(0)) if vertical else (None, None)
    sc_ref, sh_ref, g2_ref, ng_ref, wa_ref, wg_ref, cw_ref, cb_ref, wd_ref = refs[:9]
    refs = refs[9:]
    fg_ref = refs.pop(0) if final_norm else None
    out_ref, hext_ref, acc_ref = refs

    t = pl.program_id(1)
    nt = pl.num_programs(1)
    halo = width if vertical else 0
    ext = tm + 2 * halo

    def hmod(xt):
        return _rms_mod(xt, ng_ref[...], sc_ref[...], sh_ref[...]).astype(BF16)

    x = x_ref[...]
    hext_ref[halo:halo + tm, :] = hmod(x)
    if vertical:
        zeros = jnp.zeros((halo, x.shape[1]), BF16)
        hext_ref[0:halo, :] = jnp.where(t == 0, zeros, hmod(xu_ref[...]))
        hext_ref[halo + tm:ext, :] = jnp.where(t == nt - 1, zeros, hmod(xd_ref[...]))

    assert width & (width - 1) == 0
    pr = min(tm, FFN_PIECE_ROWS)
    assert pr % width == 0 and tm % pr == 0
    wpos = lax.broadcasted_iota(jnp.int32, (pr, FF_CHUNK), 0) & (width - 1)
    first_col = wpos == 0
    last_col = wpos == width - 1
    taps = (0, 1, 2) if vertical else (1,)

    def activation(c, a, g, r0):
        cw = cw_ref[c]
        u = []
        for dw in range(3):
            s = None
            for dr in taps:
                term = g[dr * halo + r0:dr * halo + r0 + pr] * cw[3 * dr + dw:3 * dr + dw + 1]
                s = term if s is None else s + term
            u.append(s)
        conv = ((u[1] + cb_ref[c]) + jnp.where(first_col, 0.0, pltpu.roll(u[0], 1, 0))
                + jnp.where(last_col, 0.0, pltpu.roll(u[2], pr - 1, 0)))
        th = jnp.tanh(conv * (GELU_K0 + GELU_K1 * (conv * conv)))
        return ((conv + conv * th) * a[r0:r0 + pr]).astype(BF16)

    def down(c0, acts):
        wd = wd_ref[c0:c0 + len(acts)].reshape(len(acts) * FF_CHUNK, wd_ref.shape[2])
        p = _dot(jnp.concatenate(acts, axis=1) if len(acts) > 1 else acts[0], wd)
        acc_ref[...] = p if c0 == 0 else acc_ref[...] + p

    state = {"a": _dot(hext_ref[halo:halo + tm, :], wa_ref[0]), "g": _dot(hext_ref[...], wg_ref[0])}
    acts, pending = [], None
    for c in range(N_FF_CHUNKS):
        a, g = state["a"], state["g"]
        matmuls = []
        if c + 1 < N_FF_CHUNKS:
            matmuls.append(lambda c=c: state.__setitem__("a", _dot(hext_ref[halo:halo + tm, :], wa_ref[c + 1])))
            matmuls.append(lambda c=c: state.__setitem__("g", _dot(hext_ref[...], wg_ref[c + 1])))
        if pending is not None:
            matmuls.append(lambda p=pending: down(*p))
            pending = None
        pieces = []
        for j in range(tm // pr):
            if matmuls:
                matmuls.pop(0)()
            pieces.append(activation(c, a, g, j * pr))
        for m in matmuls:
            m()
        acts.append(jnp.concatenate(pieces, axis=0) if len(pieces) > 1 else pieces[0])
        if len(acts) == FFN_DOWN_GROUP or c + 1 == N_FF_CHUNKS:
            pending = (c + 1 - len(acts), acts)
            acts = []
    down(*pending)
    y = x + g2_ref[...] * acc_ref[...]
    if final_norm:
        ms = jnp.mean(y * y, axis=-1, keepdims=True)
        y = y * lax.rsqrt(ms + EPS) * fg_ref[...]
    out_ref[...] = y


def _conv_ffn(x, sc, sh, g2, ng, wa, wg, cw, cb, wd, *, width, tm, vertical, final_g=None):
    bsz, seq, d = x.shape
    nt = seq // tm
    vec = pl.BlockSpec((None, 1, d), lambda b, t: (b, 0, 0))
    tile = pl.BlockSpec((None, tm, d), lambda b, t: (b, t, 0))
    in_specs = [tile]
    args = [x]
    if vertical:
        rpt = tm // width
        nrows = seq // width
        in_specs += [pl.BlockSpec((None, width, d), lambda b, t: (b, jnp.maximum(t * rpt - 1, 0), 0)),
                     pl.BlockSpec((None, width, d), lambda b, t: (b, jnp.minimum((t + 1) * rpt, nrows - 1), 0))]
        args += [x, x]
    in_specs += [vec, vec, vec, _full((1, d)), _resident(wa.shape), _resident(wg.shape),
                 _resident(cw.shape), _resident(cb.shape), _resident(wd.shape)]
    args += [sc, sh, g2, ng, wa, wg, cw, cb, wd]
    if final_g is not None:
        in_specs.append(_full((1, d)))
        args.append(final_g)
    ext = tm + (2 * width if vertical else 0)
    return pl.pallas_call(
        functools.partial(_ffn_kernel, width=width, tm=tm, vertical=vertical,
                          final_norm=final_g is not None),
        grid=(bsz, nt),
        in_specs=in_specs,
        out_specs=tile,
        out_shape=jax.ShapeDtypeStruct(x.shape, F32),
        scratch_shapes=[pltpu.VMEM((ext, d), BF16), pltpu.VMEM((tm, d), F32)],
        compiler_params=_params(("parallel", "arbitrary")),
        name="conv_ffn_grid" if vertical else "conv_ffn_seq",
    )(*args)


def _ffn_weights(w_up, conv_w, conv_b, w_down):
    d = w_up.shape[0]

    def chunked(w):
        return w.reshape(d, N_FF_CHUNKS, FF_CHUNK).transpose(1, 0, 2).astype(BF16)

    wa = chunked(w_up[:, :D_FF])
    wg = chunked(w_up[:, D_FF:])
    cw = conv_w.reshape(9, N_FF_CHUNKS, FF_CHUNK).transpose(1, 0, 2)
    cb = conv_b.reshape(N_FF_CHUNKS, 1, FF_CHUNK)
    wd = (0.5 * w_down).reshape(N_FF_CHUNKS, FF_CHUNK, d).astype(BF16)
    return wa, wg, cw, cb, wd


def kernel(x, c, ctx, c_ctx, ada_w, ada_b, norm_mix_g, norm_ffn_g, gla_w_in, gla_w_a2, gla_b_a2, gla_norm_g, gla_w_out, mlstm_w_in, mlstm_b_gate, mlstm_conv_w, mlstm_conv_b, mlstm_norm_g, mlstm_w_out, ffn_w_up, ffn_conv_w, ffn_conv_b, ffn_w_down, final_norm_g):
    bsz, seq, d = x.shape
    depth = ada_w.shape[0]
    assert bsz == 2 and d == D_MODEL and ctx.shape[1] == BLK and seq == BLK * GRID_W and depth == 2

    cvec = jnp.zeros((8, d), F32).at[:bsz].set(c).at[bsz].set(c_ctx)
    mods = _modulation(cvec, ada_w, ada_b)

    def mod(i, j):
        return mods[i, :3, j * d:(j + 1) * d].reshape(3, 1, d)

    def row(v):
        return v.reshape(1, -1)

    sh1, sc1, g1, sh2, sc2, g2 = (mod(0, j) for j in range(N_MOD))
    q, k, v, r, za = _gla_in(ctx, x, sc1, sh1, row(norm_mix_g[0]), gla_w_in[0].astype(BF16))
    o_f, o_b = _gla_scan(q, k, v, za, gla_w_a2[0], gla_b_a2[0])
    ctx, x = _gla_out(o_f, o_b, r, ctx, x, g1, row(gla_norm_g[0]), gla_w_out[0].astype(BF16))
    ffn_w = _ffn_weights(ffn_w_up[0], ffn_conv_w[0], ffn_conv_b[0], ffn_w_down[0])
    x = _conv_ffn(x, sc2[:bsz], sh2[:bsz], g2[:bsz], row(norm_ffn_g[0]), *ffn_w,
                  width=GRID_W, tm=FFN_ROWS * GRID_W, vertical=True)
    ctx = _conv_ffn(ctx, jnp.broadcast_to(sc2[2:], (bsz, 1, d)), jnp.broadcast_to(sh2[2:], (bsz, 1, d)),
                    jnp.broadcast_to(g2[2:], (bsz, 1, d)), row(norm_ffn_g[0]), *ffn_w,
                    width=BLK, tm=BLK, vertical=False)

    sh1, sc1, g1, sh2, sc2, g2 = (mod(1, j) for j in range(N_MOD))
    q, k, v, og, gt = _mlstm_in(ctx, x, sc1, sh1, row(norm_mix_g[1]), mlstm_w_in[0].astype(BF16),
                                mlstm_b_gate[0], mlstm_conv_w[0], mlstm_conv_b[0])
    h_f, h_b = _mlstm_scan(q, k, v, gt)
    x = _mlstm_out(h_f, h_b, og, x, g1[:bsz], row(mlstm_norm_g[0]), mlstm_w_out[0].astype(BF16))
    ffn_w = _ffn_weights(ffn_w_up[1], ffn_conv_w[1], ffn_conv_b[1], ffn_w_down[1])
    return _conv_ffn(x, sc2[:bsz], sh2[:bsz], g2[:bsz], row(norm_ffn_g[1]), *ffn_w,
                     width=GRID_W, tm=FFN_ROWS * GRID_W, vertical=True, final_g=row(final_norm_g))
```

```python
import functools

import jax
import jax.numpy as jnp
from jax import lax
from jax.experimental import pallas as pl
from jax.experimental.pallas import tpu as pltpu

D_MODEL = 1024
GRID_W = 64
N_HEADS = 4
D_K = 128
D_V = 256
QK_W = N_HEADS * D_K
V_W = N_HEADS * D_V
GLA_RANK = 16
GLA_TAU = 16.0
D_FF = 2816
N_MOD = 6
EPS = 1e-6

BLK = 256
SUB = 128
FF_CHUNK = 256
N_FF_CHUNKS = D_FF // FF_CHUNK
FFN_ROWS = 16
FFN_DOWN_GROUP = 2
FFN_PIECE_ROWS = 256
VMEM_LIMIT_BYTES = 56 * 1024 * 1024

F32 = jnp.float32
BF16 = jnp.bfloat16


def _dot(a, b):
    return jnp.dot(a, b, preferred_element_type=F32)


def _dot_nt(a, b):
    return lax.dot_general(a, b, (((1,), (1,)), ((), ())), preferred_element_type=F32)


def _sigmoid(x):
    return 1.0 / (1.0 + jnp.exp(-x))


def _log_sigmoid(x):
    return jnp.minimum(x, 0.0) - jnp.log(1.0 + jnp.exp(-jnp.abs(x)))


def _rms_mod(x, ng, sc, sh):
    ms = jnp.mean(x * x, axis=-1, keepdims=True)
    return (x * lax.rsqrt(ms + EPS) * ng) * (1.0 + sc) + sh


def _split_bf16(x):
    hi = x.astype(BF16)
    lo = (x - hi.astype(F32)).astype(BF16)
    return hi, lo


def _params(sem):
    return pltpu.CompilerParams(dimension_semantics=sem, vmem_limit_bytes=VMEM_LIMIT_BYTES)


def _full(shape):
    n = len(shape)
    return pl.BlockSpec(shape, lambda *_: (0,) * n)


def _resident(shape):
    n = len(shape)
    return pl.BlockSpec(shape, lambda *_: (0,) * n, pipeline_mode=pl.Buffered(1))


def _sel_spec():
    return pl.BlockSpec((None, 1, D_MODEL), lambda b, t: (jnp.where(t == 0, 2, b), 0, 0))


def _mod_kernel(c_ref, w_ref, b_ref, o_ref):
    c = c_ref[...]
    s = c * _sigmoid(c)
    o_ref[...] = jnp.dot(s, w_ref[...], preferred_element_type=F32,
                         precision=lax.Precision.HIGHEST) + b_ref[...]


def _modulation(cvec, ada_w, ada_b):
    depth = ada_w.shape[0]
    n = N_MOD * D_MODEL
    tn = 1536
    return pl.pallas_call(
        _mod_kernel,
        grid=(depth, n // tn),
        in_specs=[pl.BlockSpec((8, D_MODEL), lambda i, j: (0, 0)),
                  pl.BlockSpec((None, D_MODEL, tn), lambda i, j: (i, 0, j)),
                  pl.BlockSpec((None, 1, tn), lambda i, j: (i, 0, j))],
        out_specs=pl.BlockSpec((None, 8, tn), lambda i, j: (i, 0, j)),
        out_shape=jax.ShapeDtypeStruct((depth, 8, n), F32),
        compiler_params=_params(("parallel", "parallel")),
        name="modulation",
    )(cvec, ada_w, ada_b.reshape(depth, 1, n))


def _load_tile(ctx_ref, x_ref):
    t = pl.program_id(1)
    return jnp.where(t == 0, ctx_ref[...], x_ref[...])


def _gla_in_kernel(ctx_ref, x_ref, sc_ref, sh_ref, ng_ref, w_ref,
                   q_ref, k_ref, v_ref, r_ref, za_ref):
    h = _rms_mod(_load_tile(ctx_ref, x_ref), ng_ref[...], sc_ref[...], sh_ref[...]).astype(BF16)
    q_ref[...] = (_dot(h, w_ref[:, 0:QK_W]) * D_K ** -0.5).astype(BF16)
    k_ref[...] = _dot(h, w_ref[:, QK_W:2 * QK_W]).astype(BF16)
    v_ref[...] = _dot(h, w_ref[:, 2 * QK_W:2 * QK_W + V_W]).astype(BF16)
    r_ref[...] = _dot(h, w_ref[:, 2 * QK_W + V_W:2 * QK_W + 2 * V_W]).astype(BF16)
    za_ref[...] = _dot(h, w_ref[:, 2 * QK_W + 2 * V_W:])


def _gla_in(ctx, x, sc, sh, ng, w_in):
    bsz, seq, d = x.shape
    nblk = 1 + seq // BLK
    t_tot = nblk * BLK
    n_in = w_in.shape[1]

    def tok(width):
        return pl.BlockSpec((None, BLK, width), lambda b, t: (b, t, 0))

    return pl.pallas_call(
        _gla_in_kernel,
        grid=(bsz, nblk),
        in_specs=[pl.BlockSpec((None, BLK, d), lambda b, t: (b, 0, 0)),
                  pl.BlockSpec((None, BLK, d), lambda b, t: (b, jnp.maximum(t - 1, 0), 0)),
                  _sel_spec(), _sel_spec(), _full((1, d)), _full((d, n_in))],
        out_specs=[tok(QK_W), tok(QK_W), tok(V_W), tok(V_W), tok(2 * GLA_RANK)],
        out_shape=[jax.ShapeDtypeStruct((bsz, t_tot, QK_W), BF16),
                   jax.ShapeDtypeStruct((bsz, t_tot, QK_W), BF16),
                   jax.ShapeDtypeStruct((bsz, t_tot, V_W), BF16),
                   jax.ShapeDtypeStruct((bsz, t_tot, V_W), BF16),
                   jax.ShapeDtypeStruct((bsz, t_tot, 2 * GLA_RANK), F32)],
        compiler_params=_params(("parallel", "arbitrary")),
        name="gla_in",
    )(ctx, x, sc, sh, ng, w_in)


def _mlstm_in_kernel(ctx_ref, x_ref, sc_ref, sh_ref, ng_ref, w_ref, wgt_ref, bgt_ref, cw_ref, cb_ref,
                     q_ref, k_ref, v_ref, og_ref, gt_ref, pre_ref, last_ref, *, nblk):
    t = pl.program_id(1)

    @pl.when(t == 0)
    def _():
        last_ref[...] = jnp.zeros_like(last_ref)
        pre_ref[...] = jnp.zeros_like(pre_ref)

    h = _rms_mod(_load_tile(ctx_ref, x_ref), ng_ref[...], sc_ref[...], sh_ref[...]).astype(BF16)

    blk = t - 1
    x = pre_ref[(t + 1) % 2]
    pre = _dot(h, w_ref[:, 0:2 * QK_W])
    pre_ref[t % 2] = pre
    has_prev = blk > 1
    has_next = jnp.logical_and(blk > 0, blk != nblk - 1)
    prev = jnp.where(has_prev, last_ref[0:1, :], 0.0)
    nxt = jnp.where(has_next, pre[0:1, :], 0.0)
    row = lax.broadcasted_iota(jnp.int32, (BLK, 1), 0)
    x_m1 = jnp.where(row == 0, prev, pltpu.roll(x, 1, 0))
    x_p1 = jnp.where(row == BLK - 1, nxt, pltpu.roll(x, BLK - 1, 0))
    last_ref[0:1, :] = x[BLK - 1:BLK, :]

    def conv_piece(r0, r1):
        qk = (x_m1[r0:r1] * cw_ref[0:1, :] + x[r0:r1] * cw_ref[1:2, :] + x_p1[r0:r1] * cw_ref[2:3, :]
              + cb_ref[...])
        qk = qk * _sigmoid(qk)
        q_ref[r0:r1, :] = qk[:, :QK_W].astype(BF16)
        k_ref[r0:r1, :] = (qk[:, QK_W:] * D_K ** -0.5).astype(BF16)

    def gates():
        gt = _dot_nt(wgt_ref[...], h) + bgt_ref[...]
        is_forget = lax.broadcasted_iota(jnp.int32, gt.shape, 0) >= 2 * N_HEADS
        gt_ref[...] = jnp.where(is_forget, _log_sigmoid(gt), gt)

    n_pieces = 4
    step = BLK // n_pieces
    conv_piece(0, step)
    v_ref[...] = _dot(h, w_ref[:, 2 * QK_W:2 * QK_W + V_W]).astype(BF16)
    conv_piece(step, 2 * step)
    og_ref[...] = _dot(h, w_ref[:, 2 * QK_W + V_W:2 * QK_W + 2 * V_W]).astype(BF16)
    conv_piece(2 * step, 3 * step)
    gates()
    conv_piece(3 * step, BLK)


def _mlstm_in(ctx, x, sc, sh, ng, w_in, b_gate, conv_w, conv_b):
    bsz, seq, d = x.shape
    rows = seq // GRID_W
    assert rows == BLK
    nblk = 1 + GRID_W
    t_tot = nblk * BLK
    n_in = w_in.shape[1]
    n_g = 4 * N_HEADS
    xcol = x.reshape(bsz, rows, GRID_W * d)
    w_gt = w_in[:, 2 * QK_W + 2 * V_W:].T

    def late(width):
        return pl.BlockSpec((None, BLK, width), lambda b, t: (b, jnp.maximum(t - 1, 0), 0))

    def tok(width):
        return pl.BlockSpec((None, BLK, width), lambda b, t: (b, jnp.minimum(t, nblk - 1), 0))

    return pl.pallas_call(
        functools.partial(_mlstm_in_kernel, nblk=nblk),
        grid=(bsz, nblk + 1),
        in_specs=[pl.BlockSpec((None, BLK, d), lambda b, t: (b, 0, 0)),
                  pl.BlockSpec((None, rows, d), lambda b, t: (b, 0, jnp.clip(t - 1, 0, GRID_W - 1))),
                  _sel_spec(), _sel_spec(), _full((1, d)), _full((d, n_in)), _full((n_g, d)),
                  _full((n_g, 1)), _full((3, 2 * QK_W)), _full((1, 2 * QK_W))],
        out_specs=[late(QK_W), late(QK_W), tok(V_W), tok(V_W),
                   pl.BlockSpec((None, n_g, BLK), lambda b, t: (b, 0, jnp.minimum(t, nblk - 1)))],
        out_shape=[jax.ShapeDtypeStruct((bsz, t_tot, QK_W), BF16),
                   jax.ShapeDtypeStruct((bsz, t_tot, QK_W), BF16),
                   jax.ShapeDtypeStruct((bsz, t_tot, V_W), BF16),
                   jax.ShapeDtypeStruct((bsz, t_tot, V_W), BF16),
                   jax.ShapeDtypeStruct((bsz, n_g, t_tot), F32)],
        scratch_shapes=[pltpu.VMEM((2, BLK, 2 * QK_W), F32), pltpu.VMEM((8, 2 * QK_W), F32)],
        compiler_params=_params(("parallel", "arbitrary")),
        name="mlstm_in",
    )(ctx, xcol, sc, sh, ng, w_in, w_gt, b_gate.reshape(n_g, 1), conv_w, conv_b.reshape(1, 2 * QK_W))


def _block_tri(n, sub, lower):
    ri = lax.broadcasted_iota(jnp.int32, (n, n), 0)
    ci = lax.broadcasted_iota(jnp.int32, (n, n), 1)
    shift = sub.bit_length() - 1
    assert sub == 1 << shift
    same = (ri >> shift) == (ci >> shift)
    tri = (ci <= ri) if lower else (ci >= ri)
    return jnp.where(same & tri, 1.0, 0.0).astype(BF16)


def _tri_mask(n, lower):
    ri = lax.broadcasted_iota(jnp.int32, (n, n), 0)
    ci = lax.broadcasted_iota(jnp.int32, (n, n), 1)
    return (ci <= ri) if lower else (ci >= ri)


def _staggered_round_robin(groups, lag):
    live, rnd = [], 0
    while live or rnd <= lag * (len(groups) - 1):
        if rnd % lag == 0 and rnd // lag < len(groups):
            live = live + list(groups[rnd // lag])
        alive = []
        for gen in live:
            try:
                next(gen)
                alive.append(gen)
            except StopIteration:
                pass
        live, rnd = alive, rnd + 1


def _gla_scan_kernel(qf, kf, vf, zf, qb, kb, vb, zb, w2_ref, b2_ref, of_ref, ob_ref, *s_refs):
    @pl.when(pl.program_id(1) == 0)
    def _():
        for s_ref in s_refs:
            s_ref[...] = jnp.zeros_like(s_ref)

    dirs = ((qf, kf, vf, zf, of_ref), (qb, kb, vb, zb, ob_ref))
    tris = (_block_tri(BLK, SUB, lower=True), _block_tri(BLK, SUB, lower=False))
    masks = (_tri_mask(SUB, lower=True), _tri_mask(SUB, lower=False))

    def unit(d, h):
        q_ref, k_ref, v_ref, z_ref, o_ref = dirs[d]
        s_ref = s_refs[d * N_HEADS + h]
        fwd = d == 0
        ks = slice(h * D_K, (h + 1) * D_K)
        vs = slice(h * D_V, (h + 1) * D_V)
        za = z_ref[:, d * GLA_RANK:(d + 1) * GLA_RANK].astype(BF16)
        y = _dot(za, w2_ref[d, :, ks])
        yield
        la = _log_sigmoid(y + b2_ref[d, :, ks]) * (1.0 / GLA_TAU)
        hi, lo = _split_bf16(la)
        bcum = _dot(tris[d], jnp.concatenate([hi, lo], axis=1))
        yield
        bcum = bcum[:, :D_K] + bcum[:, D_K:]
        ops = {}
        for sub in (0, 1):
            rs = slice(sub * SUB, (sub + 1) * SUB)
            bs = bcum[rs]
            tot = bs[SUB - 1:SUB] if fwd else bs[0:1]
            mid = 0.5 * tot
            e_mid = jnp.exp(mid)
            q_mid = q_ref[rs, ks].astype(F32) * jnp.exp(bs - mid)
            k_mid = k_ref[rs, ks].astype(F32) * jnp.exp(mid - bs)
            q_in = (q_mid * e_mid).astype(BF16)
            k_st_t = (k_mid * e_mid).T.astype(BF16)
            ops[sub] = (q_mid.astype(BF16), q_in, k_mid.astype(BF16), k_st_t, e_mid * e_mid)
            yield
        for sub in ((0, 1) if fwd else (1, 0)):
            rs = slice(sub * SUB, (sub + 1) * SUB)
            q_mid, q_in, k_mid, k_st_t, decay = ops[sub]
            scores = _dot_nt(q_mid, k_mid)
            yield
            p = jnp.where(masks[d], scores, 0.0).astype(BF16)
            vh = v_ref[rs, vs]
            s_old = s_ref[...]
            o = _dot(jnp.concatenate([p, q_in], axis=1), jnp.concatenate([vh, s_old.astype(BF16)], axis=0))
            upd = _dot(k_st_t, vh)
            yield
            o_ref[rs, vs] = o.astype(o_ref.dtype)
            dcol = jnp.broadcast_to(decay, (D_K, D_K)).T
            s_ref[...] = s_old * jnp.concatenate([dcol, dcol], axis=1) + upd
            yield

    _staggered_round_robin([[unit(d, h) for h in hs for d in (0, 1)] for hs in ((0, 1), (2, 3))], lag=3)


def _scan_maps(nblk):
    def fwd(b, s):
        return (b, s, 0)

    def bwd(b, s):
        return (b, jnp.where(s == 0, 0, nblk - s), 0)

    return fwd, bwd


def _gla_scan(q, k, v, za, w_a2, b_a2):
    bsz, t_tot, _ = q.shape
    nblk = t_tot // BLK
    fwd, bwd = _scan_maps(nblk)

    def specs(m):
        return [pl.BlockSpec((None, BLK, QK_W), m), pl.BlockSpec((None, BLK, QK_W), m),
                pl.BlockSpec((None, BLK, V_W), m), pl.BlockSpec((None, BLK, 2 * GLA_RANK), m)]

    return pl.pallas_call(
        _gla_scan_kernel,
        grid=(bsz, nblk),
        in_specs=specs(fwd) + specs(bwd) + [_full((2, GLA_RANK, QK_W)), _full((2, 1, QK_W))],
        out_specs=[pl.BlockSpec((None, BLK, V_W), fwd), pl.BlockSpec((None, BLK, V_W), bwd)],
        out_shape=[jax.ShapeDtypeStruct((bsz, t_tot, V_W), BF16)] * 2,
        scratch_shapes=[pltpu.VMEM((D_K, D_V), F32)] * (2 * N_HEADS),
        compiler_params=_params(("parallel", "arbitrary")),
        name="gla_scan",
    )(q, k, v, za, q, k, v, za, w_a2.astype(BF16), b_a2.reshape(2, 1, QK_W))


def _mlstm_scan_kernel(qf, kf, vf, gtf, qb, kb, vb, gtb, of_ref, ob_ref, *state_refs):
    n_units = 2 * N_HEADS
    c_refs, nb_refs, m_refs = (state_refs[i * n_units:(i + 1) * n_units] for i in range(3))

    @pl.when(pl.program_id(1) == 0)
    def _():
        for ref in state_refs:
            ref[...] = jnp.zeros_like(ref)

    ones = jnp.ones((SUB, D_K), BF16)
    dirs = ((qf, kf, vf, gtf, of_ref), (qb, kb, vb, gtb, ob_ref))
    gates = []
    for d, (_, _, _, gt_ref, _) in enumerate(dirs):
        gt = gt_ref[...]
        gt_hi, gt_lo = _split_bf16(gt)
        tri_r = _block_tri(BLK, SUB, lower=d != 0)
        gates.append((gt, _dot(gt_hi, tri_r) + _dot(gt_lo, tri_r)))
    masks = (_tri_mask(SUB, lower=True), _tri_mask(SUB, lower=False))
    def unit(d, h):
        q_ref, k_ref, v_ref, _, o_ref = dirs[d]
        fwd = d == 0
        gt, f_row = gates[d]
        ks = slice(h * D_K, (h + 1) * D_K)
        vs = slice(h * D_V, (h + 1) * D_V)
        gi = d * N_HEADS + h
        gf_ = 2 * N_HEADS + gi
        c_ref, nb_ref, m_ref = c_refs[gi], nb_refs[gi], m_refs[gi]
        pre = {}
        for sub in (0, 1):
            rs = slice(sub * SUB, (sub + 1) * SUB)
            fc_r = f_row[gf_:gf_ + 1, rs]
            ic_r = gt[gi:gi + 1, rs]
            fc_b = jnp.broadcast_to(fc_r, (SUB, SUB)).T
            d_mat = jnp.where(masks[d], fc_b + (ic_r - fc_r), -jnp.inf)
            d_max = jnp.max(d_mat, axis=-1, keepdims=True)
            qk = _dot_nt(q_ref[rs, ks], k_ref[rs, ks])
            k_t = k_ref[rs, ks].astype(F32).T
            pre[sub] = (fc_r, ic_r, fc_b, d_mat, d_max, qk, k_t)
            yield
        for sub in ((0, 1) if fwd else (1, 0)):
            rs = slice(sub * SUB, (sub + 1) * SUB)
            fc_r, ic_r, fc_b, d_mat, d_max, qk, k_t = pre[sub]
            vh = v_ref[rs, vs]
            c_old = c_ref[...]
            n_old = nb_ref[...]
            m_old = m_ref[...]
            inter = fc_b + m_old
            m_i = jnp.maximum(inter, d_max)
            w_inter = jnp.exp(inter - m_i)
            s_mat = qk * jnp.exp(d_mat - m_i)
            lhs = jnp.concatenate([s_mat.astype(BF16),
                                   (w_inter * q_ref[rs, ks].astype(F32)).astype(BF16)], axis=1)
            num = _dot(lhs, jnp.concatenate([vh, c_old.astype(BF16)], axis=0))
            den = _dot(lhs, jnp.concatenate([ones, n_old.astype(BF16)], axis=0))
            f_last = fc_r[:, SUB - 1:SUB] if fwd else fc_r[:, 0:1]
            g_r = f_last - fc_r + ic_r
            m_new = jnp.maximum(f_last + m_old, jnp.max(g_r, axis=-1, keepdims=True))
            dec = jnp.exp(f_last + m_old - m_new)[:, 0:1]
            kw_t = (k_t * jnp.exp(g_r - m_new)).astype(BF16)
            c_upd = _dot(kw_t, vh)
            n_upd = _dot(kw_t, ones)
            yield
            r = 1.0 / jnp.maximum(jnp.abs(den), jnp.exp(-m_i))
            o_ref[rs, vs] = (num * jnp.concatenate([r, r], axis=1)).astype(o_ref.dtype)
            c_ref[...] = dec * c_old + c_upd
            nb_ref[...] = dec * n_old + n_upd
            m_ref[...] = m_new
            yield

    _staggered_round_robin([[unit(d, h) for h in hs for d in (0, 1)] for hs in ((0, 1), (2, 3))], lag=2)


def _mlstm_scan(q, k, v, gt):
    bsz, t_tot, _ = v.shape
    nblk = t_tot // BLK
    fwd, bwd = _scan_maps(nblk)
    n_g = 4 * N_HEADS

    def specs(m):
        return [pl.BlockSpec((None, BLK, QK_W), m), pl.BlockSpec((None, BLK, QK_W), m),
                pl.BlockSpec((None, BLK, V_W), m),
                pl.BlockSpec((None, n_g, BLK), lambda b, s: (b, 0, m(b, s)[1]))]

    return pl.pallas_call(
        _mlstm_scan_kernel,
        grid=(bsz, nblk),
        in_specs=specs(fwd) + specs(bwd),
        out_specs=[pl.BlockSpec((None, BLK, V_W), fwd), pl.BlockSpec((None, BLK, V_W), bwd)],
        out_shape=[jax.ShapeDtypeStruct((bsz, t_tot, V_W), BF16)] * 2,
        scratch_shapes=([pltpu.VMEM((D_K, D_V), F32)] * (2 * N_HEADS)
                        + [pltpu.VMEM((D_K, D_K), F32)] * (2 * N_HEADS)
                        + [pltpu.VMEM((1, D_K), F32)] * (2 * N_HEADS)),
        compiler_params=_params(("parallel", "arbitrary")),
        name="mlstm_scan",
    )(q, k, v, gt, q, k, v, gt)


OUT_PIECES = 2


def _head_norm_gate(of_ref, ob_ref, gate, ng_ref, rows):
    o = of_ref[rows, :].astype(F32) + ob_ref[rows, :].astype(F32)
    parts = []
    for h in range(N_HEADS):
        oh = o[:, h * D_V:(h + 1) * D_V]
        ms = jnp.mean(oh * oh, axis=-1, keepdims=True)
        parts.append(oh * lax.rsqrt(ms + EPS))
    return (jnp.concatenate(parts, axis=1) * ng_ref[...] * gate).astype(BF16)


def _mixer_out_pieces(gated, w_ref, finish):
    pr = BLK // OUT_PIECES
    rows = [slice(j * pr, (j + 1) * pr) for j in range(OUT_PIECES)]
    nxt = gated(rows[0])
    for j in range(OUT_PIECES):
        y = _dot(nxt, w_ref[...])
        if j + 1 < OUT_PIECES:
            nxt = gated(rows[j + 1])
        finish(rows[j], y)


def _gla_out_kernel(of_ref, ob_ref, r_ref, ctx_ref, x_ref, g1_ref, ng_ref, w_ref,
                    ctx_out, x_out):
    t = pl.program_id(1)

    def gated(rows):
        r = r_ref[rows, :].astype(F32)
        return _head_norm_gate(of_ref, ob_ref, r * _sigmoid(r), ng_ref, rows)

    ys = []

    def finish(rows, y):
        x_out[rows, :] = x_ref[rows, :] + g1_ref[...] * y
        ys.append((rows, y))

    _mixer_out_pieces(gated, w_ref, finish)

    @pl.when(t == 0)
    def _():
        for rows, y in ys:
            ctx_out[rows, :] = ctx_ref[rows, :] + g1_ref[...] * y


def _gla_out(o_f, o_b, r, ctx, x, g1, ng, w_out):
    bsz, seq, d = x.shape
    nblk = 1 + seq // BLK
    tok = pl.BlockSpec((None, BLK, V_W), lambda b, t: (b, t, 0))
    ctx_spec = pl.BlockSpec((None, BLK, d), lambda b, t: (b, 0, 0))
    x_spec = pl.BlockSpec((None, BLK, d), lambda b, t: (b, jnp.maximum(t - 1, 0), 0))
    return pl.pallas_call(
        _gla_out_kernel,
        grid=(bsz, nblk),
        in_specs=[tok, tok, tok, ctx_spec, x_spec, _sel_spec(), _full((1, V_W)), _full((V_W, d))],
        out_specs=[ctx_spec, x_spec],
        out_shape=[jax.ShapeDtypeStruct(ctx.shape, F32), jax.ShapeDtypeStruct(x.shape, F32)],
        compiler_params=_params(("parallel", "arbitrary")),
        name="gla_out",
    )(o_f, o_b, r, ctx, x, g1, ng, w_out)


def _mlstm_out_kernel(of_ref, ob_ref, og_ref, x_ref, g1_ref, ng_ref, w_ref, x_out):
    def gated(rows):
        return _head_norm_gate(of_ref, ob_ref, _sigmoid(og_ref[rows, :].astype(F32)), ng_ref, rows)

    def finish(rows, y):
        x_out[rows, :] = x_ref[rows, :] + g1_ref[...] * y

    _mixer_out_pieces(gated, w_ref, finish)


def _mlstm_out(h_f, h_b, og, x, g1, ng, w_out):
    bsz, seq, d = x.shape
    rows = seq // GRID_W
    xcol = x.reshape(bsz, rows, GRID_W * d)
    tok = pl.BlockSpec((None, BLK, V_W), lambda b, c: (b, c + 1, 0))
    col = pl.BlockSpec((None, rows, d), lambda b, c: (b, 0, c))
    out = pl.pallas_call(
        _mlstm_out_kernel,
        grid=(bsz, GRID_W),
        in_specs=[tok, tok, tok, col,
                  pl.BlockSpec((None, 1, d), lambda b, c: (b, 0, 0)),
                  _full((1, V_W)), _full((V_W, d))],
        out_specs=col,
        out_shape=jax.ShapeDtypeStruct(xcol.shape, F32),
        compiler_params=_params(("parallel", "arbitrary")),
        name="mlstm_out",
    )(h_f, h_b, og, xcol, g1, ng, w_out)
    return out.reshape(bsz, seq, d)


GELU_K0 = 0.7978845608028654
GELU_K1 = GELU_K0 * 0.044715


def _ffn_kernel(*refs, width, tm, vertical, final_norm):
    refs = list(refs)
    x_ref = refs.pop(0)
    xu_ref, xd_ref = (refs.pop(0), refs.pop(0)) if vertical else (None, None)
    sc_ref, sh_ref, g2_ref, ng_ref, wa_ref, wg_ref, cw_ref, cb_ref, wd_ref = refs[:9]
    refs = refs[9:]
    fg_ref = refs.pop(0) if final_norm else None
    out_ref, hext_ref, acc_ref = refs

    t = pl.program_id(1)
    nt = pl.num_programs(1)
    halo = width if vertical else 0
    ext = tm + 2 * halo

    def hmod(xt):
        return _rms_mod(xt, ng_ref[...], sc_ref[...], sh_ref[...]).astype(BF16)

    assert width & (width - 1) == 0
    pr = min(tm, FFN_PIECE_ROWS)
    assert pr % width == 0 and tm % pr == 0
    n_pieces = tm // pr
    wpos = lax.broadcasted_iota(jnp.int32, (pr, FF_CHUNK), 0) & (width - 1)
    first_col = wpos == 0
    last_col = wpos == width - 1
    taps = (0, 1, 2) if vertical else (1,)

    def chunk(c):
        return slice(c * FF_CHUNK, (c + 1) * FF_CHUNK)

    def activation(c, a, g, r0):
        cw = cw_ref[:, chunk(c)]
        u = []
        for dw in range(3):
            s = None
            for dr in taps:
                term = g[dr * halo + r0:dr * halo + r0 + pr] * cw[3 * dr + dw:3 * dr + dw + 1]
                s = term if s is None else s + term
            u.append(s)
        conv = ((u[1] + cb_ref[:, chunk(c)]) + jnp.where(first_col, 0.0, pltpu.roll(u[0], 1, 0))
                + jnp.where(last_col, 0.0, pltpu.roll(u[2], pr - 1, 0)))
        th = jnp.tanh(conv * (GELU_K0 + GELU_K1 * (conv * conv)))
        return ((conv + conv * th) * a[r0:r0 + pr]).astype(BF16)

    def down(c0, acts):
        wd = wd_ref[c0 * FF_CHUNK:(c0 + len(acts)) * FF_CHUNK, :]
        p = _dot(jnp.concatenate(acts, axis=1) if len(acts) > 1 else acts[0], wd)
        acc_ref[...] = p if c0 == 0 else acc_ref[...] + p

    def up_a(c):
        return _dot(hext_ref[halo:halo + tm, :], wa_ref[:, chunk(c)])

    def up_g(c):
        return _dot(hext_ref[...], wg_ref[:, chunk(c)])

    x = x_ref[...]
    hext_ref[halo:halo + tm, :] = hmod(x)
    if vertical:
        zeros = jnp.zeros((halo, x.shape[1]), BF16)
        hext_ref[0:halo, :] = jnp.where(t == 0, zeros, hmod(xu_ref[...]))
        hext_ref[halo + tm:ext, :] = jnp.where(t == nt - 1, zeros, hmod(xd_ref[...]))

    state = {"a": up_a(0), "g": up_g(0)}
    acts, pending = [], None
    for c in range(N_FF_CHUNKS):
        a, g = state["a"], state["g"]
        matmuls = []
        if c + 1 < N_FF_CHUNKS:
            matmuls.append(lambda c=c: state.__setitem__("a", up_a(c + 1)))
            matmuls.append(lambda c=c: state.__setitem__("g", up_g(c + 1)))
        if pending is not None:
            matmuls.append(lambda p=pending: down(*p))
            pending = None
        pieces = []
        for j in range(n_pieces):
            if matmuls:
                matmuls.pop(0)()
            pieces.append(activation(c, a, g, j * pr))
        for m in matmuls:
            m()
        acts.append(jnp.concatenate(pieces, axis=0) if len(pieces) > 1 else pieces[0])
        if len(acts) == FFN_DOWN_GROUP or c + 1 == N_FF_CHUNKS:
            pending = (c + 1 - len(acts), acts)
            acts = []
    down(*pending)
    y = x + g2_ref[...] * acc_ref[...]
    if final_norm:
        ms = jnp.mean(y * y, axis=-1, keepdims=True)
        y = y * lax.rsqrt(ms + EPS) * fg_ref[...]
    out_ref[...] = y


def _conv_ffn(x, sc, sh, g2, ng, wa, wg, cw, cb, wd, *, width, tm, vertical, final_g=None):
    bsz, seq, d = x.shape
    nt = seq // tm
    vec = pl.BlockSpec((None, 1, d), lambda b, t: (b, 0, 0))
    tile = pl.BlockSpec((None, tm, d), lambda b, t: (b, t, 0))
    in_specs = [tile]
    args = [x]
    if vertical:
        rpt = tm // width
        nrows = seq // width
        in_specs += [pl.BlockSpec((None, width, d), lambda b, t: (b, jnp.maximum(t * rpt - 1, 0), 0)),
                     pl.BlockSpec((None, width, d), lambda b, t: (b, jnp.minimum((t + 1) * rpt, nrows - 1), 0))]
        args += [x, x]
    in_specs += [vec, vec, vec, _full((1, d)), _resident(wa.shape), _resident(wg.shape),
                 _resident(cw.shape), _resident(cb.shape), _resident(wd.shape)]
    args += [sc, sh, g2, ng, wa, wg, cw, cb, wd]
    if final_g is not None:
        in_specs.append(_full((1, d)))
        args.append(final_g)
    ext = tm + (2 * width if vertical else 0)
    return pl.pallas_call(
        functools.partial(_ffn_kernel, width=width, tm=tm, vertical=vertical,
                          final_norm=final_g is not None),
        grid=(bsz, nt),
        in_specs=in_specs,
        out_specs=tile,
        out_shape=jax.ShapeDtypeStruct(x.shape, F32),
        scratch_shapes=[pltpu.VMEM((ext, d), BF16), pltpu.VMEM((tm, d), F32)],
        compiler_params=_params(("parallel", "arbitrary")),
        name="conv_ffn_grid" if vertical else "conv_ffn_seq",
    )(*args)


def _ffn_weights(w_up, conv_w, conv_b, w_down):
    d = w_up.shape[0]

    assert w_up.shape == (d, 2 * D_FF)
    wa = w_up[:, :D_FF].astype(BF16)
    wg = w_up[:, D_FF:].astype(BF16)
    cw = conv_w.reshape(9, D_FF)
    cb = conv_b.reshape(1, D_FF)
    wd = (0.5 * w_down).astype(BF16)
    return wa, wg, cw, cb, wd


def kernel(x, c, ctx, c_ctx, ada_w, ada_b, norm_mix_g, norm_ffn_g, gla_w_in, gla_w_a2, gla_b_a2, gla_norm_g, gla_w_out, mlstm_w_in, mlstm_b_gate, mlstm_conv_w, mlstm_conv_b, mlstm_norm_g, mlstm_w_out, ffn_w_up, ffn_conv_w, ffn_conv_b, ffn_w_down, final_norm_g):
    bsz, seq, d = x.shape
    depth = ada_w.shape[0]
    assert bsz == 2 and d == D_MODEL and ctx.shape[1] == BLK and seq == BLK * GRID_W and depth == 2

    cvec = jnp.zeros((8, d), F32).at[:bsz].set(c).at[bsz].set(c_ctx)
    mods = _modulation(cvec, ada_w, ada_b)

    def mod(i, j):
        return mods[i, :3, j * d:(j + 1) * d].reshape(3, 1, d)

    def row(v):
        return v.reshape(1, -1)

    sh1, sc1, g1, sh2, sc2, g2 = (mod(0, j) for j in range(N_MOD))
    q, k, v, r, za = _gla_in(ctx, x, sc1, sh1, row(norm_mix_g[0]), gla_w_in[0].astype(BF16))
    o_f, o_b = _gla_scan(q, k, v, za, gla_w_a2[0], gla_b_a2[0])
    ctx, x = _gla_out(o_f, o_b, r, ctx, x, g1, row(gla_norm_g[0]), gla_w_out[0].astype(BF16))
    ffn_w = _ffn_weights(ffn_w_up[0], ffn_conv_w[0], ffn_conv_b[0], ffn_w_down[0])
    x = _conv_ffn(x, sc2[:bsz], sh2[:bsz], g2[:bsz], row(norm_ffn_g[0]), *ffn_w,
                  width=GRID_W, tm=FFN_ROWS * GRID_W, vertical=True)
    ctx = _conv_ffn(ctx, jnp.broadcast_to(sc2[2:], (bsz, 1, d)), jnp.broadcast_to(sh2[2:], (bsz, 1, d)),
                    jnp.broadcast_to(g2[2:], (bsz, 1, d)), row(norm_ffn_g[0]), *ffn_w,
                    width=BLK, tm=BLK, vertical=False)

    sh1, sc1, g1, sh2, sc2, g2 = (mod(1, j) for j in range(N_MOD))
    q, k, v, og, gt = _mlstm_in(ctx, x, sc1, sh1, row(norm_mix_g[1]), mlstm_w_in[0].astype(BF16),
                                mlstm_b_gate[0], mlstm_conv_w[0], mlstm_conv_b[0])
    h_f, h_b = _mlstm_scan(q, k, v, gt)
    x = _mlstm_out(h_f, h_b, og, x, g1[:bsz], row(mlstm_norm_g[0]), mlstm_w_out[0].astype(BF16))
    ffn_w = _ffn_weights(ffn_w_up[1], ffn_conv_w[1], ffn_conv_b[1], ffn_w_down[1])
    return _conv_ffn(x, sc2[:bsz], sh2[:bsz], g2[:bsz], row(norm_ffn_g[1]), *ffn_w,
                     width=GRID_W, tm=FFN_ROWS * GRID_W, vertical=True, final_g=row(final_norm_g))
```

```python
import functools

import jax
import jax.numpy as jnp
from jax import lax
from jax.experimental import pallas as pl
from jax.experimental.pallas import tpu as pltpu

D_MODEL = 1024
GRID_W = 64
N_HEADS = 4
D_K = 128
D_V = 256
QK_W = N_HEADS * D_K
V_W = N_HEADS * D_V
GLA_RANK = 16
GLA_TAU = 16.0
D_FF = 2816
N_MOD = 6
EPS = 1e-6

BLK = 256
SUB = 128
FF_CHUNK = 256
N_FF_CHUNKS = D_FF // FF_CHUNK
FFN_ROWS = 16
FFN_DOWN_GROUP = 2
FFN_PIECE_ROWS = 256
VMEM_LIMIT_BYTES = 62 * 1024 * 1024

F32 = jnp.float32
BF16 = jnp.bfloat16


def _dot(a, b):
    return jnp.dot(a, b, preferred_element_type=F32)


def _dot_nt(a, b):
    return lax.dot_general(a, b, (((1,), (1,)), ((), ())), preferred_element_type=F32)


def _sigmoid(x):
    return 1.0 / (1.0 + jnp.exp(-x))


def _log_sigmoid(x):
    return jnp.minimum(x, 0.0) - jnp.log(1.0 + jnp.exp(-jnp.abs(x)))


def _rms_mod(x, ng, sc, sh):
    ms = jnp.mean(x * x, axis=-1, keepdims=True)
    return (x * lax.rsqrt(ms + EPS) * ng) * (1.0 + sc) + sh


def _split_bf16(x):
    hi = x.astype(BF16)
    lo = (x - hi.astype(F32)).astype(BF16)
    return hi, lo


def _params(sem):
    return pltpu.CompilerParams(dimension_semantics=sem, vmem_limit_bytes=VMEM_LIMIT_BYTES)


def _full(shape):
    n = len(shape)
    return pl.BlockSpec(shape, lambda *_: (0,) * n)


def _resident(shape):
    n = len(shape)
    return pl.BlockSpec(shape, lambda *_: (0,) * n, pipeline_mode=pl.Buffered(1))


def _sel_spec():
    return pl.BlockSpec((None, 1, D_MODEL), lambda b, t: (jnp.where(t == 0, 2, b), 0, 0))


def _mod_kernel(c_ref, w_ref, b_ref, o_ref):
    c = c_ref[...]
    s = c * _sigmoid(c)
    o_ref[...] = jnp.dot(s, w_ref[...], preferred_element_type=F32,
                         precision=lax.Precision.HIGHEST) + b_ref[...]


def _modulation(cvec, ada_w, ada_b):
    depth = ada_w.shape[0]
    n = N_MOD * D_MODEL
    tn = 1536
    return pl.pallas_call(
        _mod_kernel,
        grid=(depth, n // tn),
        in_specs=[pl.BlockSpec((8, D_MODEL), lambda i, j: (0, 0)),
                  pl.BlockSpec((None, D_MODEL, tn), lambda i, j: (i, 0, j)),
                  pl.BlockSpec((None, 1, tn), lambda i, j: (i, 0, j))],
        out_specs=pl.BlockSpec((None, 8, tn), lambda i, j: (i, 0, j)),
        out_shape=jax.ShapeDtypeStruct((depth, 8, n), F32),
        compiler_params=_params(("parallel", "parallel")),
        name="modulation",
    )(cvec, ada_w, ada_b.reshape(depth, 1, n))


def _load_tile(ctx_ref, x_ref):
    t = pl.program_id(1)
    return jnp.where(t == 0, ctx_ref[...], x_ref[...])


def _gla_in_kernel(ctx_ref, x_ref, sc_ref, sh_ref, ng_ref, w_ref,
                   q_ref, k_ref, v_ref, r_ref, za_ref):
    h = _rms_mod(_load_tile(ctx_ref, x_ref), ng_ref[...], sc_ref[...], sh_ref[...]).astype(BF16)
    q_ref[...] = (_dot(h, w_ref[:, 0:QK_W]) * D_K ** -0.5).astype(BF16)
    k_ref[...] = _dot(h, w_ref[:, QK_W:2 * QK_W]).astype(BF16)
    v_ref[...] = _dot(h, w_ref[:, 2 * QK_W:2 * QK_W + V_W]).astype(BF16)
    r_ref[...] = _dot(h, w_ref[:, 2 * QK_W + V_W:2 * QK_W + 2 * V_W]).astype(BF16)
    za_ref[...] = _dot(h, w_ref[:, 2 * QK_W + 2 * V_W:])


def _gla_in(ctx, x, sc, sh, ng, w_in):
    bsz, seq, d = x.shape
    nblk = 1 + seq // BLK
    t_tot = nblk * BLK
    n_in = w_in.shape[1]

    def tok(width):
        return pl.BlockSpec((None, BLK, width), lambda b, t: (b, t, 0))

    return pl.pallas_call(
        _gla_in_kernel,
        grid=(bsz, nblk),
        in_specs=[pl.BlockSpec((None, BLK, d), lambda b, t: (b, 0, 0)),
                  pl.BlockSpec((None, BLK, d), lambda b, t: (b, jnp.maximum(t - 1, 0), 0)),
                  _sel_spec(), _sel_spec(), _full((1, d)), _full((d, n_in))],
        out_specs=[tok(QK_W), tok(QK_W), tok(V_W), tok(V_W), tok(2 * GLA_RANK)],
        out_shape=[jax.ShapeDtypeStruct((bsz, t_tot, QK_W), BF16),
                   jax.ShapeDtypeStruct((bsz, t_tot, QK_W), BF16),
                   jax.ShapeDtypeStruct((bsz, t_tot, V_W), BF16),
                   jax.ShapeDtypeStruct((bsz, t_tot, V_W), BF16),
                   jax.ShapeDtypeStruct((bsz, t_tot, 2 * GLA_RANK), F32)],
        compiler_params=_params(("parallel", "arbitrary")),
        name="gla_in",
    )(ctx, x, sc, sh, ng, w_in)


def _mlstm_in_kernel(ctx_ref, hl_ref, sc_ref, sh_ref, ng_ref, w_ref, wgt_ref, bgt_ref, cw_ref, cb_ref,
                     q_ref, k_ref, v_ref, og_ref, gt_ref, pre_ref, last_ref, h_ref, *, nblk):
    t = pl.program_id(1)

    @pl.when(t == 0)
    def _():
        last_ref[...] = jnp.zeros_like(last_ref)
        pre_ref[...] = jnp.zeros_like(pre_ref)
        h_ref[...] = _rms_mod(ctx_ref[...], ng_ref[...], sc_ref[...], sh_ref[...]).astype(BF16)

    @pl.when(t != 0)
    def _():
        h_ref[...] = hl_ref[...]

    h = h_ref[...]

    blk = t - 1
    x = pre_ref[(t + 1) % 2]
    pre = _dot(h, w_ref[:, 0:2 * QK_W])
    pre_ref[t % 2] = pre
    has_prev = blk > 1
    has_next = jnp.logical_and(blk > 0, blk != nblk - 1)
    prev = jnp.where(has_prev, last_ref[0:1, :], 0.0)
    nxt = jnp.where(has_next, pre[0:1, :], 0.0)
    row = lax.broadcasted_iota(jnp.int32, (BLK, 1), 0)
    x_m1 = jnp.where(row == 0, prev, pltpu.roll(x, 1, 0))
    x_p1 = jnp.where(row == BLK - 1, nxt, pltpu.roll(x, BLK - 1, 0))
    last_ref[0:1, :] = x[BLK - 1:BLK, :]

    def conv_piece(r0, r1):
        qk = (x_m1[r0:r1] * cw_ref[0:1, :] + x[r0:r1] * cw_ref[1:2, :] + x_p1[r0:r1] * cw_ref[2:3, :]
              + cb_ref[...])
        qk = qk * _sigmoid(qk)
        q_ref[r0:r1, :] = qk[:, :QK_W].astype(BF16)
        k_ref[r0:r1, :] = (qk[:, QK_W:] * D_K ** -0.5).astype(BF16)

    def gates():
        gt = _dot_nt(wgt_ref[...], h) + bgt_ref[...]
        is_forget = lax.broadcasted_iota(jnp.int32, gt.shape, 0) >= 2 * N_HEADS
        gt_ref[...] = jnp.where(is_forget, _log_sigmoid(gt), gt)

    n_pieces = 4
    step = BLK // n_pieces
    conv_piece(0, step)
    v_ref[...] = _dot(h, w_ref[:, 2 * QK_W:2 * QK_W + V_W]).astype(BF16)
    conv_piece(step, 2 * step)
    og_ref[...] = _dot(h, w_ref[:, 2 * QK_W + V_W:2 * QK_W + 2 * V_W]).astype(BF16)
    conv_piece(2 * step, 3 * step)
    gates()
    conv_piece(3 * step, BLK)


def _mlstm_in(ctx, h_lat, sc, sh, ng, w_in, b_gate, conv_w, conv_b):
    bsz, seq, d = h_lat.shape
    rows = seq // GRID_W
    assert rows == BLK
    nblk = 1 + GRID_W
    t_tot = nblk * BLK
    n_in = w_in.shape[1]
    n_g = 4 * N_HEADS
    xcol = h_lat.reshape(bsz, rows, GRID_W * d)
    w_gt = w_in[:, 2 * QK_W + 2 * V_W:].T

    def late(width):
        return pl.BlockSpec((None, BLK, width), lambda b, t: (b, jnp.maximum(t - 1, 0), 0))

    def tok(width):
        return pl.BlockSpec((None, BLK, width), lambda b, t: (b, jnp.minimum(t, nblk - 1), 0))

    return pl.pallas_call(
        functools.partial(_mlstm_in_kernel, nblk=nblk),
        grid=(bsz, nblk + 1),
        in_specs=[pl.BlockSpec((None, BLK, d), lambda b, t: (b, 0, 0)),
                  pl.BlockSpec((None, rows, d), lambda b, t: (b, 0, jnp.clip(t - 1, 0, GRID_W - 1))),
                  pl.BlockSpec((None, 1, d), lambda b, t: (0, 0, 0)),
                  pl.BlockSpec((None, 1, d), lambda b, t: (0, 0, 0)),
                  _full((1, d)), _full((d, n_in)), _full((n_g, d)),
                  _full((n_g, 1)), _full((3, 2 * QK_W)), _full((1, 2 * QK_W))],
        out_specs=[late(QK_W), late(QK_W), tok(V_W), tok(V_W),
                   pl.BlockSpec((None, n_g, BLK), lambda b, t: (b, 0, jnp.minimum(t, nblk - 1)))],
        out_shape=[jax.ShapeDtypeStruct((bsz, t_tot, QK_W), BF16),
                   jax.ShapeDtypeStruct((bsz, t_tot, QK_W), BF16),
                   jax.ShapeDtypeStruct((bsz, t_tot, V_W), BF16),
                   jax.ShapeDtypeStruct((bsz, t_tot, V_W), BF16),
                   jax.ShapeDtypeStruct((bsz, n_g, t_tot), F32)],
        scratch_shapes=[pltpu.VMEM((2, BLK, 2 * QK_W), F32), pltpu.VMEM((8, 2 * QK_W), F32),
                        pltpu.VMEM((BLK, d), BF16)],
        compiler_params=_params(("parallel", "arbitrary")),
        name="mlstm_in",
    )(ctx, xcol, sc, sh, ng, w_in, w_gt, b_gate.reshape(n_g, 1), conv_w, conv_b.reshape(1, 2 * QK_W))


def _block_tri(n, sub, lower):
    ri = lax.broadcasted_iota(jnp.int32, (n, n), 0)
    ci = lax.broadcasted_iota(jnp.int32, (n, n), 1)
    shift = sub.bit_length() - 1
    assert sub == 1 << shift
    same = (ri >> shift) == (ci >> shift)
    tri = (ci <= ri) if lower else (ci >= ri)
    return jnp.where(same & tri, 1.0, 0.0).astype(BF16)


def _tri_mask(n, lower):
    ri = lax.broadcasted_iota(jnp.int32, (n, n), 0)
    ci = lax.broadcasted_iota(jnp.int32, (n, n), 1)
    return (ci <= ri) if lower else (ci >= ri)


def _staggered_round_robin(groups, lag):
    live, rnd = [], 0
    while live or rnd <= lag * (len(groups) - 1):
        if rnd % lag == 0 and rnd // lag < len(groups):
            live = live + list(groups[rnd // lag])
        alive = []
        for gen in live:
            try:
                next(gen)
                alive.append(gen)
            except StopIteration:
                pass
        live, rnd = alive, rnd + 1


def _gla_scan_kernel(qf, kf, vf, zf, qb, kb, vb, zb, w2_ref, b2_ref, of_ref, ob_ref, *s_refs):
    @pl.when(pl.program_id(1) == 0)
    def _():
        for s_ref in s_refs:
            s_ref[...] = jnp.zeros_like(s_ref)

    dirs = ((qf, kf, vf, zf, of_ref), (qb, kb, vb, zb, ob_ref))
    tris = (_block_tri(BLK, SUB, lower=True), _block_tri(BLK, SUB, lower=False))
    masks = (_tri_mask(SUB, lower=True), _tri_mask(SUB, lower=False))

    def unit(d, h):
        q_ref, k_ref, v_ref, z_ref, o_ref = dirs[d]
        s_ref = s_refs[d * N_HEADS + h]
        fwd = d == 0
        ks = slice(h * D_K, (h + 1) * D_K)
        vs = slice(h * D_V, (h + 1) * D_V)
        za = z_ref[:, d * GLA_RANK:(d + 1) * GLA_RANK].astype(BF16)
        y = _dot(za, w2_ref[d, :, ks])
        yield
        la = _log_sigmoid(y + b2_ref[d, :, ks]) * (1.0 / GLA_TAU)
        hi, lo = _split_bf16(la)
        bcum = _dot(tris[d], jnp.concatenate([hi, lo], axis=1))
        yield
        bcum = bcum[:, :D_K] + bcum[:, D_K:]
        ops = {}
        for sub in (0, 1):
            rs = slice(sub * SUB, (sub + 1) * SUB)
            bs = bcum[rs]
            tot = bs[SUB - 1:SUB] if fwd else bs[0:1]
            mid = 0.5 * tot
            e_mid = jnp.exp(mid)
            q_mid = q_ref[rs, ks].astype(F32) * jnp.exp(bs - mid)
            k_mid = k_ref[rs, ks].astype(F32) * jnp.exp(mid - bs)
            q_in = (q_mid * e_mid).astype(BF16)
            k_st_t = (k_mid * e_mid).T.astype(BF16)
            ops[sub] = (q_mid.astype(BF16), q_in, k_mid.astype(BF16), k_st_t, e_mid * e_mid)
            yield
        for sub in ((0, 1) if fwd else (1, 0)):
            rs = slice(sub * SUB, (sub + 1) * SUB)
            q_mid, q_in, k_mid, k_st_t, decay = ops[sub]
            scores = _dot_nt(q_mid, k_mid)
            yield
            p = jnp.where(masks[d], scores, 0.0).astype(BF16)
            vh = v_ref[rs, vs]
            s_old = s_ref[...]
            o = _dot(jnp.concatenate([p, q_in], axis=1), jnp.concatenate([vh, s_old.astype(BF16)], axis=0))
            upd = _dot(k_st_t, vh)
            yield
            o_ref[rs, vs] = o.astype(o_ref.dtype)
            dcol = jnp.broadcast_to(decay, (D_K, D_K)).T
            s_ref[...] = s_old * jnp.concatenate([dcol, dcol], axis=1) + upd
            yield

    _staggered_round_robin([[unit(d, h) for h in hs for d in (0, 1)] for hs in ((0, 1), (2, 3))], lag=3)


def _scan_maps(nblk):
    def fwd(b, s):
        return (b, s, 0)

    def bwd(b, s):
        return (b, jnp.where(s == 0, 0, nblk - s), 0)

    return fwd, bwd


def _gla_scan(q, k, v, za, w_a2, b_a2):
    bsz, t_tot, _ = q.shape
    nblk = t_tot // BLK
    fwd, bwd = _scan_maps(nblk)

    def specs(m):
        return [pl.BlockSpec((None, BLK, QK_W), m), pl.BlockSpec((None, BLK, QK_W), m),
                pl.BlockSpec((None, BLK, V_W), m), pl.BlockSpec((None, BLK, 2 * GLA_RANK), m)]

    return pl.pallas_call(
        _gla_scan_kernel,
        grid=(bsz, nblk),
        in_specs=specs(fwd) + specs(bwd) + [_full((2, GLA_RANK, QK_W)), _full((2, 1, QK_W))],
        out_specs=[pl.BlockSpec((None, BLK, V_W), fwd), pl.BlockSpec((None, BLK, V_W), bwd)],
        out_shape=[jax.ShapeDtypeStruct((bsz, t_tot, V_W), BF16)] * 2,
        scratch_shapes=[pltpu.VMEM((D_K, D_V), F32)] * (2 * N_HEADS),
        compiler_params=_params(("parallel", "arbitrary")),
        name="gla_scan",
    )(q, k, v, za, q, k, v, za, w_a2.astype(BF16), b_a2.reshape(2, 1, QK_W))


def _mlstm_scan_kernel(qf, kf, vf, gtf, qb, kb, vb, gtb, of_ref, ob_ref, *state_refs):
    n_units = 2 * N_HEADS
    c_refs, nb_refs, m_refs = (state_refs[i * n_units:(i + 1) * n_units] for i in range(3))

    @pl.when(pl.program_id(1) == 0)
    def _():
        for ref in state_refs:
            ref[...] = jnp.zeros_like(ref)

    ones = jnp.ones((SUB, D_K), BF16)
    dirs = ((qf, kf, vf, gtf, of_ref), (qb, kb, vb, gtb, ob_ref))
    gates = []
    for d, (_, _, _, gt_ref, _) in enumerate(dirs):
        gt = gt_ref[...]
        gt_hi, gt_lo = _split_bf16(gt)
        tri_r = _block_tri(BLK, SUB, lower=d != 0)
        gates.append((gt, _dot(gt_hi, tri_r) + _dot(gt_lo, tri_r)))
    masks = (_tri_mask(SUB, lower=True), _tri_mask(SUB, lower=False))
    def unit(d, h):
        q_ref, k_ref, v_ref, _, o_ref = dirs[d]
        fwd = d == 0
        gt, f_row = gates[d]
        ks = slice(h * D_K, (h + 1) * D_K)
        vs = slice(h * D_V, (h + 1) * D_V)
        gi = d * N_HEADS + h
        gf_ = 2 * N_HEADS + gi
        c_ref, nb_ref, m_ref = c_refs[gi], nb_refs[gi], m_refs[gi]
        pre = {}
        for sub in (0, 1):
            rs = slice(sub * SUB, (sub + 1) * SUB)
            fc_r = f_row[gf_:gf_ + 1, rs]
            ic_r = gt[gi:gi + 1, rs]
            fc_b = jnp.broadcast_to(fc_r, (SUB, SUB)).T
            d_mat = jnp.where(masks[d], fc_b + (ic_r - fc_r), -jnp.inf)
            d_max = jnp.max(d_mat, axis=-1, keepdims=True)
            qk = _dot_nt(q_ref[rs, ks], k_ref[rs, ks])
            k_t = k_ref[rs, ks].astype(F32).T
            pre[sub] = (fc_r, ic_r, fc_b, d_mat, d_max, qk, k_t)
            yield
        for sub in ((0, 1) if fwd else (1, 0)):
            rs = slice(sub * SUB, (sub + 1) * SUB)
            fc_r, ic_r, fc_b, d_mat, d_max, qk, k_t = pre[sub]
            vh = v_ref[rs, vs]
            c_old = c_ref[...]
            n_old = nb_ref[...]
            m_old = m_ref[...]
            inter = fc_b + m_old
            m_i = jnp.maximum(inter, d_max)
            w_inter = jnp.exp(inter - m_i)
            s_mat = qk * jnp.exp(d_mat - m_i)
            lhs = jnp.concatenate([s_mat.astype(BF16),
                                   (w_inter * q_ref[rs, ks].astype(F32)).astype(BF16)], axis=1)
            num = _dot(lhs, jnp.concatenate([vh, c_old.astype(BF16)], axis=0))
            den = _dot(lhs, jnp.concatenate([ones, n_old.astype(BF16)], axis=0))
            f_last = fc_r[:, SUB - 1:SUB] if fwd else fc_r[:, 0:1]
            g_r = f_last - fc_r + ic_r
            m_new = jnp.maximum(f_last + m_old, jnp.max(g_r, axis=-1, keepdims=True))
            dec = jnp.exp(f_last + m_old - m_new)[:, 0:1]
            kw_t = (k_t * jnp.exp(g_r - m_new)).astype(BF16)
            c_upd = _dot(kw_t, vh)
            n_upd = _dot(kw_t, ones)
            yield
            r = 1.0 / jnp.maximum(jnp.abs(den), jnp.exp(-m_i))
            o_ref[rs, vs] = (num * jnp.concatenate([r, r], axis=1)).astype(o_ref.dtype)
            c_ref[...] = dec * c_old + c_upd
            nb_ref[...] = dec * n_old + n_upd
            m_ref[...] = m_new
            yield

    _staggered_round_robin([[unit(d, h) for h in hs for d in (0, 1)] for hs in ((0, 1), (2, 3))], lag=2)


def _mlstm_scan(q, k, v, gt):
    bsz, t_tot, _ = v.shape
    nblk = t_tot // BLK
    fwd, bwd = _scan_maps(nblk)
    n_g = 4 * N_HEADS

    def specs(m):
        return [pl.BlockSpec((None, BLK, QK_W), m), pl.BlockSpec((None, BLK, QK_W), m),
                pl.BlockSpec((None, BLK, V_W), m),
                pl.BlockSpec((None, n_g, BLK), lambda b, s: (b, 0, m(b, s)[1]))]

    return pl.pallas_call(
        _mlstm_scan_kernel,
        grid=(bsz, nblk),
        in_specs=specs(fwd) + specs(bwd),
        out_specs=[pl.BlockSpec((None, BLK, V_W), fwd), pl.BlockSpec((None, BLK, V_W), bwd)],
        out_shape=[jax.ShapeDtypeStruct((bsz, t_tot, V_W), BF16)] * 2,
        scratch_shapes=([pltpu.VMEM((D_K, D_V), F32)] * (2 * N_HEADS)
                        + [pltpu.VMEM((D_K, D_K), F32)] * (2 * N_HEADS)
                        + [pltpu.VMEM((1, D_K), F32)] * (2 * N_HEADS)),
        compiler_params=_params(("parallel", "arbitrary")),
        name="mlstm_scan",
    )(q, k, v, gt, q, k, v, gt)


OUT_PIECES = 2


def _head_norm_gate(of_ref, ob_ref, gate, ng_ref, rows):
    o = of_ref[rows, :].astype(F32) + ob_ref[rows, :].astype(F32)
    parts = []
    for h in range(N_HEADS):
        oh = o[:, h * D_V:(h + 1) * D_V]
        ms = jnp.mean(oh * oh, axis=-1, keepdims=True)
        parts.append(oh * lax.rsqrt(ms + EPS))
    return (jnp.concatenate(parts, axis=1) * ng_ref[...] * gate).astype(BF16)


def _mixer_out_pieces(gated, w_ref, finish):
    pr = BLK // OUT_PIECES
    rows = [slice(j * pr, (j + 1) * pr) for j in range(OUT_PIECES)]
    nxt = gated(rows[0])
    for j in range(OUT_PIECES):
        y = _dot(nxt, w_ref[...])
        if j + 1 < OUT_PIECES:
            nxt = gated(rows[j + 1])
        finish(rows[j], y)


def _gla_out_kernel(of_ref, ob_ref, r_ref, ctx_ref, g1_ref, ng_ref, w_ref, ctx_out, delta_out):
    t = pl.program_id(1)

    def gated(rows):
        r = r_ref[rows, :].astype(F32)
        return _head_norm_gate(of_ref, ob_ref, r * _sigmoid(r), ng_ref, rows)

    ys = []

    def finish(rows, y):
        delta_out[rows, :] = (g1_ref[...] * y).astype(delta_out.dtype)
        ys.append((rows, y))

    _mixer_out_pieces(gated, w_ref, finish)

    @pl.when(t == 0)
    def _():
        for rows, y in ys:
            ctx_out[rows, :] = ctx_ref[rows, :] + g1_ref[...] * y


def _gla_out(o_f, o_b, r, ctx, g1, ng, w_out):
    bsz, t_tot, _ = o_f.shape
    d = ctx.shape[2]
    nblk = t_tot // BLK
    tok = pl.BlockSpec((None, BLK, V_W), lambda b, t: (b, t, 0))
    ctx_spec = pl.BlockSpec((None, BLK, d), lambda b, t: (b, 0, 0))
    lat_spec = pl.BlockSpec((None, BLK, d), lambda b, t: (b, jnp.maximum(t - 1, 0), 0))
    return pl.pallas_call(
        _gla_out_kernel,
        grid=(bsz, nblk),
        in_specs=[tok, tok, tok, ctx_spec, _sel_spec(), _full((1, V_W)), _full((V_W, d))],
        out_specs=[ctx_spec, lat_spec],
        out_shape=[jax.ShapeDtypeStruct(ctx.shape, F32),
                   jax.ShapeDtypeStruct((bsz, t_tot - BLK, d), BF16)],
        compiler_params=_params(("parallel", "arbitrary")),
        name="gla_out",
    )(o_f, o_b, r, ctx, g1, ng, w_out)


def _mlstm_out_kernel(of_ref, ob_ref, og_ref, g1_ref, ng_ref, w_ref, delta_out):
    def gated(rows):
        return _head_norm_gate(of_ref, ob_ref, _sigmoid(og_ref[rows, :].astype(F32)), ng_ref, rows)

    def finish(rows, y):
        delta_out[rows, :] = (g1_ref[...] * y).astype(delta_out.dtype)

    _mixer_out_pieces(gated, w_ref, finish)


def _mlstm_out(h_f, h_b, og, g1, ng, w_out):
    bsz, t_tot, _ = h_f.shape
    d = w_out.shape[1]
    seq = t_tot - BLK
    rows = seq // GRID_W
    tok = pl.BlockSpec((None, BLK, V_W), lambda b, c: (b, c + 1, 0))
    out = pl.pallas_call(
        _mlstm_out_kernel,
        grid=(bsz, GRID_W),
        in_specs=[tok, tok, tok,
                  pl.BlockSpec((None, 1, d), lambda b, c: (b, 0, 0)),
                  _full((1, V_W)), _full((V_W, d))],
        out_specs=pl.BlockSpec((None, rows, d), lambda b, c: (b, 0, c)),
        out_shape=jax.ShapeDtypeStruct((bsz, rows, GRID_W * d), BF16),
        compiler_params=_params(("parallel", "arbitrary")),
        name="mlstm_out",
    )(h_f, h_b, og, g1, ng, w_out)
    return out.reshape(bsz, seq, d)


GELU_K0 = 0.7978845608028654
GELU_K1 = GELU_K0 * 0.044715


def _ffn_kernel(*refs, width, tm, vertical, has_delta, final_norm, next_h):
    refs = list(refs)
    x_ref = refs.pop(0)
    xu_ref, xd_ref = (refs.pop(0), refs.pop(0)) if vertical else (None, None)
    d_ref, du_ref, dd_ref = (refs.pop(0), refs.pop(0), refs.pop(0)) if has_delta else (None,) * 3
    sc_ref, sh_ref, g2_ref, ng_ref, wa_ref, wg_ref, cw_ref, cb_ref, wd_ref = refs[:9]
    refs = refs[9:]
    fg_ref = refs.pop(0) if final_norm else None
    nsc_ref, nsh_ref, nng_ref = (refs.pop(0), refs.pop(0), refs.pop(0)) if next_h else (None,) * 3
    out_ref = refs.pop(0)
    hn_ref = refs.pop(0) if next_h else None
    hext_ref, acc_ref = refs

    def residual(x_r, delta_r):
        return x_r[...] + delta_r[...].astype(F32) if has_delta else x_r[...]

    t = pl.program_id(1)
    nt = pl.num_programs(1)
    halo = width if vertical else 0
    ext = tm + 2 * halo

    def hmod(xt):
        return _rms_mod(xt, ng_ref[...], sc_ref[...], sh_ref[...]).astype(BF16)

    assert width & (width - 1) == 0
    pr = min(tm, FFN_PIECE_ROWS)
    assert pr % width == 0 and tm % pr == 0
    n_pieces = tm // pr
    wpos = lax.broadcasted_iota(jnp.int32, (pr, FF_CHUNK), 0) & (width - 1)
    first_col = wpos == 0
    last_col = wpos == width - 1
    taps = (0, 1, 2) if vertical else (1,)

    def chunk(c):
        return slice(c * FF_CHUNK, (c + 1) * FF_CHUNK)

    def activation(c, a, g, r0):
        cw = cw_ref[:, chunk(c)]
        u = []
        for dw in range(3):
            s = None
            for dr in taps:
                term = g[dr * halo + r0:dr * halo + r0 + pr] * cw[3 * dr + dw:3 * dr + dw + 1]
                s = term if s is None else s + term
            u.append(s)
        conv = ((u[1] + cb_ref[:, chunk(c)]) + jnp.where(first_col, 0.0, pltpu.roll(u[0], 1, 0))
                + jnp.where(last_col, 0.0, pltpu.roll(u[2], pr - 1, 0)))
        th = jnp.tanh(conv * (GELU_K0 + GELU_K1 * (conv * conv)))
        return ((conv + conv * th) * a[r0:r0 + pr]).astype(BF16)

    def down(c0, acts):
        wd = wd_ref[c0 * FF_CHUNK:(c0 + len(acts)) * FF_CHUNK, :]
        p = _dot(jnp.concatenate(acts, axis=1) if len(acts) > 1 else acts[0], wd)
        acc_ref[...] = p if c0 == 0 else acc_ref[...] + p

    def up_a(c):
        return _dot(hext_ref[halo:halo + tm, :], wa_ref[:, chunk(c)])

    def up_g(c):
        return _dot(hext_ref[...], wg_ref[:, chunk(c)])

    out_ref[...] = residual(x_ref, d_ref)
    hext_ref[halo:halo + tm, :] = hmod(out_ref[...])
    if vertical:
        zeros = jnp.zeros((halo, hext_ref.shape[1]), BF16)
        hext_ref[0:halo, :] = jnp.where(t == 0, zeros, hmod(residual(xu_ref, du_ref)))
        hext_ref[halo + tm:ext, :] = jnp.where(t == nt - 1, zeros, hmod(residual(xd_ref, dd_ref)))

    state = {"a": up_a(0), "g": up_g(0)}
    acts, pending = [], None
    for c in range(N_FF_CHUNKS):
        a, g = state["a"], state["g"]
        matmuls = []
        if c + 1 < N_FF_CHUNKS:
            matmuls.append(lambda c=c: state.__setitem__("a", up_a(c + 1)))
            matmuls.append(lambda c=c: state.__setitem__("g", up_g(c + 1)))
        if pending is not None:
            matmuls.append(lambda p=pending: down(*p))
            pending = None
        pieces = []
        for j in range(n_pieces):
            if matmuls:
                matmuls.pop(0)()
            pieces.append(activation(c, a, g, j * pr))
        for m in matmuls:
            m()
        acts.append(jnp.concatenate(pieces, axis=0) if len(pieces) > 1 else pieces[0])
        if len(acts) == FFN_DOWN_GROUP or c + 1 == N_FF_CHUNKS:
            pending = (c + 1 - len(acts), acts)
            acts = []
    down(*pending)
    y = out_ref[...] + g2_ref[...] * acc_ref[...]
    if next_h:
        hn_ref[...] = _rms_mod(y, nng_ref[...], nsc_ref[...], nsh_ref[...]).astype(hn_ref.dtype)
    if final_norm:
        ms = jnp.mean(y * y, axis=-1, keepdims=True)
        y = y * lax.rsqrt(ms + EPS) * fg_ref[...]
    out_ref[...] = y


def _conv_ffn(x, delta, sc, sh, g2, ng, wa, wg, cw, cb, wd, *, width, tm, vertical,
              final_g=None, next_mod=None):
    bsz, seq, d = x.shape
    nt = seq // tm
    vec = pl.BlockSpec((None, 1, d), lambda b, t: (b, 0, 0))
    tile = pl.BlockSpec((None, tm, d), lambda b, t: (b, t, 0))
    halo_specs = []
    if vertical:
        rpt = tm // width
        nrows = seq // width
        halo_specs = [pl.BlockSpec((None, width, d), lambda b, t: (b, jnp.maximum(t * rpt - 1, 0), 0)),
                      pl.BlockSpec((None, width, d), lambda b, t: (b, jnp.minimum((t + 1) * rpt, nrows - 1), 0))]
    in_specs = [tile] + halo_specs
    args = [x] * (1 + len(halo_specs))
    if delta is not None:
        assert vertical
        in_specs += [tile] + halo_specs
        args += [delta] * 3
    in_specs += [vec, vec, vec, _full((1, d)), _resident(wa.shape), _resident(wg.shape),
                 _resident(cw.shape), _resident(cb.shape), _resident(wd.shape)]
    args += [sc, sh, g2, ng, wa, wg, cw, cb, wd]
    if final_g is not None:
        in_specs.append(_full((1, d)))
        args.append(final_g)
    out_specs, out_shape = tile, jax.ShapeDtypeStruct(x.shape, F32)
    if next_mod is not None:
        in_specs += [vec, vec, _full((1, d))]
        args += list(next_mod)
        out_specs, out_shape = [tile, tile], [out_shape, jax.ShapeDtypeStruct(x.shape, BF16)]
    ext = tm + (2 * width if vertical else 0)
    return pl.pallas_call(
        functools.partial(_ffn_kernel, width=width, tm=tm, vertical=vertical,
                          has_delta=delta is not None, final_norm=final_g is not None,
                          next_h=next_mod is not None),
        grid=(bsz, nt),
        in_specs=in_specs,
        out_specs=out_specs,
        out_shape=out_shape,
        scratch_shapes=[pltpu.VMEM((ext, d), BF16), pltpu.VMEM((tm, d), F32)],
        compiler_params=_params(("parallel", "arbitrary")),
        name="conv_ffn_grid" if vertical else "conv_ffn_seq",
    )(*args)


def _ffn_weights(w_up, conv_w, conv_b, w_down):
    d = w_up.shape[0]

    assert w_up.shape == (d, 2 * D_FF)
    wa = w_up[:, :D_FF].astype(BF16)
    wg = w_up[:, D_FF:].astype(BF16)
    cw = conv_w.reshape(9, D_FF)
    cb = conv_b.reshape(1, D_FF)
    wd = (0.5 * w_down).astype(BF16)
    return wa, wg, cw, cb, wd


def kernel(x, c, ctx, c_ctx, ada_w, ada_b, norm_mix_g, norm_ffn_g, gla_w_in, gla_w_a2, gla_b_a2, gla_norm_g, gla_w_out, mlstm_w_in, mlstm_b_gate, mlstm_conv_w, mlstm_conv_b, mlstm_norm_g, mlstm_w_out, ffn_w_up, ffn_conv_w, ffn_conv_b, ffn_w_down, final_norm_g):
    bsz, seq, d = x.shape
    depth = ada_w.shape[0]
    assert bsz == 2 and d == D_MODEL and ctx.shape[1] == BLK and seq == BLK * GRID_W and depth == 2

    cvec = jnp.zeros((8, d), F32).at[:bsz].set(c).at[bsz].set(c_ctx)
    mods = _modulation(cvec, ada_w, ada_b)

    def mod(i, j):
        return mods[i, :3, j * d:(j + 1) * d].reshape(3, 1, d)

    def row(v):
        return v.reshape(1, -1)

    sh1, sc1, g1, sh2, sc2, g2 = (mod(0, j) for j in range(N_MOD))
    nsh1, nsc1, ng1, nsh2, nsc2, ng2 = (mod(1, j) for j in range(N_MOD))
    q, k, v, r, za = _gla_in(ctx, x, sc1, sh1, row(norm_mix_g[0]), gla_w_in[0].astype(BF16))
    o_f, o_b = _gla_scan(q, k, v, za, gla_w_a2[0], gla_b_a2[0])
    ctx, delta = _gla_out(o_f, o_b, r, ctx, g1, row(gla_norm_g[0]), gla_w_out[0].astype(BF16))
    ffn_w = _ffn_weights(ffn_w_up[0], ffn_conv_w[0], ffn_conv_b[0], ffn_w_down[0])
    x, h_lat = _conv_ffn(x, delta, sc2[:bsz], sh2[:bsz], g2[:bsz], row(norm_ffn_g[0]), *ffn_w,
                         width=GRID_W, tm=FFN_ROWS * GRID_W, vertical=True,
                         next_mod=(nsc1[:bsz], nsh1[:bsz], row(norm_mix_g[1])))
    ctx = _conv_ffn(ctx, None, jnp.broadcast_to(sc2[2:], (bsz, 1, d)), jnp.broadcast_to(sh2[2:], (bsz, 1, d)),
                    jnp.broadcast_to(g2[2:], (bsz, 1, d)), row(norm_ffn_g[0]), *ffn_w,
                    width=BLK, tm=BLK, vertical=False)

    q, k, v, og, gt = _mlstm_in(ctx, h_lat, nsc1[2:], nsh1[2:], row(norm_mix_g[1]), mlstm_w_in[0].astype(BF16),
                                mlstm_b_gate[0], mlstm_conv_w[0], mlstm_conv_b[0])
    h_f, h_b = _mlstm_scan(q, k, v, gt)
    delta = _mlstm_out(h_f, h_b, og, ng1[:bsz], row(mlstm_norm_g[0]), mlstm_w_out[0].astype(BF16))
    ffn_w = _ffn_weights(ffn_w_up[1], ffn_conv_w[1], ffn_conv_b[1], ffn_w_down[1])
    return _conv_ffn(x, delta, nsc2[:bsz], nsh2[:bsz], ng2[:bsz], row(norm_ffn_g[1]), *ffn_w,
                     width=GRID_W, tm=FFN_ROWS * GRID_W, vertical=True, final_g=row(final_norm_g))
```

```python
import functools

import jax
import jax.numpy as jnp
from jax import lax
from jax.experimental import pallas as pl
from jax.experimental.pallas import tpu as pltpu

D_MODEL = 1024
GRID_W = 64
N_HEADS = 4
D_K = 128
D_V = 256
QK_W = N_HEADS * D_K
V_W = N_HEADS * D_V
GLA_RANK = 16
GLA_TAU = 16.0
D_FF = 2816
N_MOD = 6
EPS = 1e-6

BLK = 256
SUB = 128
FF_CHUNK = 256
N_FF_CHUNKS = D_FF // FF_CHUNK
FFN_ROWS = 16
FFN_PIECE_ROWS = 256
VMEM_LIMIT_BYTES = 62 * 1024 * 1024

F32 = jnp.float32
BF16 = jnp.bfloat16


def _dot(a, b):
    return jnp.dot(a, b, preferred_element_type=F32)


def _dot_nt(a, b):
    return lax.dot_general(a, b, (((1,), (1,)), ((), ())), preferred_element_type=F32)


def _sigmoid(x):
    return 1.0 / (1.0 + jnp.exp(-x))


def _log_sigmoid(x):
    return jnp.minimum(x, 0.0) - jnp.log(1.0 + jnp.exp(-jnp.abs(x)))


def _rms_mod(x, ng, sc, sh):
    ms = jnp.mean(x * x, axis=-1, keepdims=True)
    return (x * lax.rsqrt(ms + EPS) * ng) * (1.0 + sc) + sh


def _split_bf16(x):
    hi = x.astype(BF16)
    lo = (x - hi.astype(F32)).astype(BF16)
    return hi, lo


def _params(sem):
    return pltpu.CompilerParams(dimension_semantics=sem, vmem_limit_bytes=VMEM_LIMIT_BYTES)


def _full(shape):
    n = len(shape)
    return pl.BlockSpec(shape, lambda *_: (0,) * n)


def _resident(shape):
    n = len(shape)
    return pl.BlockSpec(shape, lambda *_: (0,) * n, pipeline_mode=pl.Buffered(1))


def _sel_spec():
    return pl.BlockSpec((None, 1, D_MODEL), lambda b, t: (jnp.where(t == 0, 2, b), 0, 0))


def _mod_kernel(c_ref, w_ref, b_ref, o_ref):
    c = c_ref[...]
    s = c * _sigmoid(c)
    o_ref[...] = jnp.dot(s, w_ref[...], preferred_element_type=F32,
                         precision=lax.Precision.HIGHEST) + b_ref[...]


def _modulation(cvec, ada_w, ada_b):
    depth = ada_w.shape[0]
    n = N_MOD * D_MODEL
    tn = 1536
    return pl.pallas_call(
        _mod_kernel,
        grid=(depth, n // tn),
        in_specs=[pl.BlockSpec((8, D_MODEL), lambda i, j: (0, 0)),
                  pl.BlockSpec((None, D_MODEL, tn), lambda i, j: (i, 0, j)),
                  pl.BlockSpec((None, 1, tn), lambda i, j: (i, 0, j))],
        out_specs=pl.BlockSpec((None, 8, tn), lambda i, j: (i, 0, j)),
        out_shape=jax.ShapeDtypeStruct((depth, 8, n), F32),
        compiler_params=_params(("parallel", "parallel")),
        name="modulation",
    )(cvec, ada_w, ada_b.reshape(depth, 1, n))


def _load_tile(ctx_ref, x_ref):
    t = pl.program_id(1)
    return jnp.where(t == 0, ctx_ref[...], x_ref[...])


def _gla_in_kernel(ctx_ref, x_ref, sc_ref, sh_ref, ng_ref, w_ref,
                   q_ref, k_ref, v_ref, r_ref, za_ref):
    h = _rms_mod(_load_tile(ctx_ref, x_ref), ng_ref[...], sc_ref[...], sh_ref[...]).astype(BF16)
    q_ref[...] = (_dot(h, w_ref[:, 0:QK_W]) * D_K ** -0.5).astype(BF16)
    k_ref[...] = _dot(h, w_ref[:, QK_W:2 * QK_W]).astype(BF16)
    v_ref[...] = _dot(h, w_ref[:, 2 * QK_W:2 * QK_W + V_W]).astype(BF16)
    r_ref[...] = _dot(h, w_ref[:, 2 * QK_W + V_W:2 * QK_W + 2 * V_W]).astype(BF16)
    za_ref[...] = _dot(h, w_ref[:, 2 * QK_W + 2 * V_W:])


def _gla_in(ctx, x, sc, sh, ng, w_in):
    bsz, seq, d = x.shape
    nblk = 1 + seq // BLK
    t_tot = nblk * BLK
    n_in = w_in.shape[1]

    def tok(width):
        return pl.BlockSpec((None, BLK, width), lambda b, t: (b, t, 0))

    return pl.pallas_call(
        _gla_in_kernel,
        grid=(bsz, nblk),
        in_specs=[pl.BlockSpec((None, BLK, d), lambda b, t: (b, 0, 0)),
                  pl.BlockSpec((None, BLK, d), lambda b, t: (b, jnp.maximum(t - 1, 0), 0)),
                  _sel_spec(), _sel_spec(), _full((1, d)), _full((d, n_in))],
        out_specs=[tok(QK_W), tok(QK_W), tok(V_W), tok(V_W), tok(2 * GLA_RANK)],
        out_shape=[jax.ShapeDtypeStruct((bsz, t_tot, QK_W), BF16),
                   jax.ShapeDtypeStruct((bsz, t_tot, QK_W), BF16),
                   jax.ShapeDtypeStruct((bsz, t_tot, V_W), BF16),
                   jax.ShapeDtypeStruct((bsz, t_tot, V_W), BF16),
                   jax.ShapeDtypeStruct((bsz, t_tot, 2 * GLA_RANK), F32)],
        compiler_params=_params(("parallel", "arbitrary")),
        name="gla_in",
    )(ctx, x, sc, sh, ng, w_in)


def _mlstm_in_kernel(ctx_ref, x_ref, sc_ref, sh_ref, ng_ref, w_ref, wgt_ref, bgt_ref, cw_ref, cb_ref,
                     q_ref, k_ref, v_ref, og_ref, gt_ref, pre_ref, last_ref, *, nblk):
    t = pl.program_id(1)

    @pl.when(t == 0)
    def _():
        last_ref[...] = jnp.zeros_like(last_ref)
        pre_ref[...] = jnp.zeros_like(pre_ref)

    h = _rms_mod(_load_tile(ctx_ref, x_ref), ng_ref[...], sc_ref[...], sh_ref[...]).astype(BF16)

    blk = t - 1
    x = pre_ref[(t + 1) % 2]
    pre = _dot(h, w_ref[:, 0:2 * QK_W])
    pre_ref[t % 2] = pre
    has_prev = blk > 1
    has_next = jnp.logical_and(blk > 0, blk != nblk - 1)
    prev = jnp.where(has_prev, last_ref[0:1, :], 0.0)
    nxt = jnp.where(has_next, pre[0:1, :], 0.0)
    row = lax.broadcasted_iota(jnp.int32, (BLK, 1), 0)
    x_m1 = jnp.where(row == 0, prev, pltpu.roll(x, 1, 0))
    x_p1 = jnp.where(row == BLK - 1, nxt, pltpu.roll(x, BLK - 1, 0))
    last_ref[0:1, :] = x[BLK - 1:BLK, :]

    def conv_piece(r0, r1):
        qk = (x_m1[r0:r1] * cw_ref[0:1, :] + x[r0:r1] * cw_ref[1:2, :] + x_p1[r0:r1] * cw_ref[2:3, :]
              + cb_ref[...])
        qk = qk * _sigmoid(qk)
        q_ref[r0:r1, :] = qk[:, :QK_W].astype(BF16)
        k_ref[r0:r1, :] = (qk[:, QK_W:] * D_K ** -0.5).astype(BF16)

    def gates():
        gt = _dot_nt(wgt_ref[...], h) + bgt_ref[...]
        is_forget = lax.broadcasted_iota(jnp.int32, gt.shape, 0) >= 2 * N_HEADS
        gt_ref[...] = jnp.where(is_forget, _log_sigmoid(gt), gt)

    n_pieces = 4
    step = BLK // n_pieces
    conv_piece(0, step)
    v_ref[...] = _dot(h, w_ref[:, 2 * QK_W:2 * QK_W + V_W]).astype(BF16)
    conv_piece(step, 2 * step)
    og_ref[...] = _dot(h, w_ref[:, 2 * QK_W + V_W:2 * QK_W + 2 * V_W]).astype(BF16)
    conv_piece(2 * step, 3 * step)
    gates()
    conv_piece(3 * step, BLK)


def _mlstm_in(ctx, x, sc, sh, ng, w_in, b_gate, conv_w, conv_b):
    bsz, seq, d = x.shape
    rows = seq // GRID_W
    assert rows == BLK
    nblk = 1 + GRID_W
    t_tot = nblk * BLK
    n_in = w_in.shape[1]
    n_g = 4 * N_HEADS
    xcol = x.reshape(bsz, rows, GRID_W * d)
    w_gt = w_in[:, 2 * QK_W + 2 * V_W:].T

    def late(width):
        return pl.BlockSpec((None, BLK, width), lambda b, t: (b, jnp.maximum(t - 1, 0), 0))

    def tok(width):
        return pl.BlockSpec((None, BLK, width), lambda b, t: (b, jnp.minimum(t, nblk - 1), 0))

    return pl.pallas_call(
        functools.partial(_mlstm_in_kernel, nblk=nblk),
        grid=(bsz, nblk + 1),
        in_specs=[pl.BlockSpec((None, BLK, d), lambda b, t: (b, 0, 0)),
                  pl.BlockSpec((None, rows, d), lambda b, t: (b, 0, jnp.clip(t - 1, 0, GRID_W - 1))),
                  _sel_spec(), _sel_spec(), _full((1, d)), _full((d, n_in)), _full((n_g, d)),
                  _full((n_g, 1)), _full((3, 2 * QK_W)), _full((1, 2 * QK_W))],
        out_specs=[late(QK_W), late(QK_W), tok(V_W), tok(V_W),
                   pl.BlockSpec((None, n_g, BLK), lambda b, t: (b, 0, jnp.minimum(t, nblk - 1)))],
        out_shape=[jax.ShapeDtypeStruct((bsz, t_tot, QK_W), BF16),
                   jax.ShapeDtypeStruct((bsz, t_tot, QK_W), BF16),
                   jax.ShapeDtypeStruct((bsz, t_tot, V_W), BF16),
                   jax.ShapeDtypeStruct((bsz, t_tot, V_W), BF16),
                   jax.ShapeDtypeStruct((bsz, n_g, t_tot), F32)],
        scratch_shapes=[pltpu.VMEM((2, BLK, 2 * QK_W), F32), pltpu.VMEM((8, 2 * QK_W), F32)],
        compiler_params=_params(("parallel", "arbitrary")),
        name="mlstm_in",
    )(ctx, xcol, sc, sh, ng, w_in, w_gt, b_gate.reshape(n_g, 1), conv_w, conv_b.reshape(1, 2 * QK_W))


def _block_tri(n, sub, lower):
    ri = lax.broadcasted_iota(jnp.int32, (n, n), 0)
    ci = lax.broadcasted_iota(jnp.int32, (n, n), 1)
    shift = sub.bit_length() - 1
    assert sub == 1 << shift
    same = (ri >> shift) == (ci >> shift)
    tri = (ci <= ri) if lower else (ci >= ri)
    return jnp.where(same & tri, 1.0, 0.0).astype(BF16)


def _tri_mask(n, lower):
    ri = lax.broadcasted_iota(jnp.int32, (n, n), 0)
    ci = lax.broadcasted_iota(jnp.int32, (n, n), 1)
    return (ci <= ri) if lower else (ci >= ri)


def _staggered_round_robin(groups, lag):
    live, rnd = [], 0
    while live or rnd <= lag * (len(groups) - 1):
        if rnd % lag == 0 and rnd // lag < len(groups):
            live = live + list(groups[rnd // lag])
        alive = []
        for gen in live:
            try:
                next(gen)
                alive.append(gen)
            except StopIteration:
                pass
        live, rnd = alive, rnd + 1


def _gla_scan_kernel(qf, kf, vf, zf, qb, kb, vb, zb, w2_ref, b2_ref, of_ref, ob_ref, *s_refs):
    @pl.when(pl.program_id(1) == 0)
    def _():
        for s_ref in s_refs:
            s_ref[...] = jnp.zeros_like(s_ref)

    dirs = ((qf, kf, vf, zf, of_ref), (qb, kb, vb, zb, ob_ref))
    tris = (_block_tri(BLK, SUB, lower=True), _block_tri(BLK, SUB, lower=False))
    masks = (_tri_mask(SUB, lower=True), _tri_mask(SUB, lower=False))

    def unit(d, h):
        q_ref, k_ref, v_ref, z_ref, o_ref = dirs[d]
        s_ref = s_refs[d * N_HEADS + h]
        fwd = d == 0
        ks = slice(h * D_K, (h + 1) * D_K)
        vs = slice(h * D_V, (h + 1) * D_V)
        za = z_ref[:, d * GLA_RANK:(d + 1) * GLA_RANK].astype(BF16)
        y = _dot(za, w2_ref[d, :, ks])
        yield
        la = _log_sigmoid(y + b2_ref[d, :, ks]) * (1.0 / GLA_TAU)
        hi, lo = _split_bf16(la)
        bcum = _dot(tris[d], jnp.concatenate([hi, lo], axis=1))
        yield
        bcum = bcum[:, :D_K] + bcum[:, D_K:]
        ops = {}
        for sub in (0, 1):
            rs = slice(sub * SUB, (sub + 1) * SUB)
            bs = bcum[rs]
            tot = bs[SUB - 1:SUB] if fwd else bs[0:1]
            mid = 0.5 * tot
            e_mid = jnp.exp(mid)
            q_mid = q_ref[rs, ks].astype(F32) * jnp.exp(bs - mid)
            k_mid = k_ref[rs, ks].astype(F32) * jnp.exp(mid - bs)
            q_in = (q_mid * e_mid).astype(BF16)
            k_st_t = (k_mid * e_mid).T.astype(BF16)
            ops[sub] = (q_mid.astype(BF16), q_in, k_mid.astype(BF16), k_st_t, e_mid * e_mid)
            yield
        for sub in ((0, 1) if fwd else (1, 0)):
            rs = slice(sub * SUB, (sub + 1) * SUB)
            q_mid, q_in, k_mid, k_st_t, decay = ops[sub]
            scores = _dot_nt(q_mid, k_mid)
            yield
            p = jnp.where(masks[d], scores, 0.0).astype(BF16)
            vh = v_ref[rs, vs]
            s_old = s_ref[...]
            o = _dot(jnp.concatenate([p, q_in], axis=1), jnp.concatenate([vh, s_old.astype(BF16)], axis=0))
            upd = _dot(k_st_t, vh)
            yield
            o_ref[rs, vs] = o.astype(o_ref.dtype)
            dcol = jnp.broadcast_to(decay, (D_K, D_K)).T
            s_ref[...] = s_old * jnp.concatenate([dcol, dcol], axis=1) + upd
            yield

    _staggered_round_robin([[unit(d, h) for h in hs for d in (0, 1)] for hs in ((0, 1), (2, 3))], lag=3)


def _scan_maps(nblk):
    def fwd(b, s):
        return (b, s, 0)

    def bwd(b, s):
        return (b, jnp.where(s == 0, 0, nblk - s), 0)

    return fwd, bwd


def _gla_scan(q, k, v, za, w_a2, b_a2):
    bsz, t_tot, _ = q.shape
    nblk = t_tot // BLK
    fwd, bwd = _scan_maps(nblk)

    def specs(m):
        return [pl.BlockSpec((None, BLK, QK_W), m), pl.BlockSpec((None, BLK, QK_W), m),
                pl.BlockSpec((None, BLK, V_W), m), pl.BlockSpec((None, BLK, 2 * GLA_RANK), m)]

    return pl.pallas_call(
        _gla_scan_kernel,
        grid=(bsz, nblk),
        in_specs=specs(fwd) + specs(bwd) + [_full((2, GLA_RANK, QK_W)), _full((2, 1, QK_W))],
        out_specs=[pl.BlockSpec((None, BLK, V_W), fwd), pl.BlockSpec((None, BLK, V_W), bwd)],
        out_shape=[jax.ShapeDtypeStruct((bsz, t_tot, V_W), BF16)] * 2,
        scratch_shapes=[pltpu.VMEM((D_K, D_V), F32)] * (2 * N_HEADS),
        compiler_params=_params(("parallel", "arbitrary")),
        name="gla_scan",
    )(q, k, v, za, q, k, v, za, w_a2.astype(BF16), b_a2.reshape(2, 1, QK_W))


def _mlstm_scan_kernel(qf, kf, vf, gtf, qb, kb, vb, gtb, of_ref, ob_ref, *state_refs):
    n_units = 2 * N_HEADS
    c_refs, nb_refs, m_refs = (state_refs[i * n_units:(i + 1) * n_units] for i in range(3))

    @pl.when(pl.program_id(1) == 0)
    def _():
        for ref in state_refs:
            ref[...] = jnp.zeros_like(ref)

    ones = jnp.ones((SUB, D_K), BF16)
    dirs = ((qf, kf, vf, gtf, of_ref), (qb, kb, vb, gtb, ob_ref))
    gates = []
    for d, (_, _, _, gt_ref, _) in enumerate(dirs):
        gt = gt_ref[...]
        gt_hi, gt_lo = _split_bf16(gt)
        tri_r = _block_tri(BLK, SUB, lower=d != 0)
        gates.append((gt, _dot(gt_hi, tri_r) + _dot(gt_lo, tri_r)))
    masks = (_tri_mask(SUB, lower=True), _tri_mask(SUB, lower=False))
    def unit(d, h):
        q_ref, k_ref, v_ref, _, o_ref = dirs[d]
        fwd = d == 0
        gt, f_row = gates[d]
        ks = slice(h * D_K, (h + 1) * D_K)
        vs = slice(h * D_V, (h + 1) * D_V)
        gi = d * N_HEADS + h
        gf_ = 2 * N_HEADS + gi
        c_ref, nb_ref, m_ref = c_refs[gi], nb_refs[gi], m_refs[gi]
        pre = {}
        for sub in (0, 1):
            rs = slice(sub * SUB, (sub + 1) * SUB)
            fc_r = f_row[gf_:gf_ + 1, rs]
            ic_r = gt[gi:gi + 1, rs]
            fc_b = jnp.broadcast_to(fc_r, (SUB, SUB)).T
            d_mat = jnp.where(masks[d], fc_b + (ic_r - fc_r), -jnp.inf)
            d_max = jnp.max(d_mat, axis=-1, keepdims=True)
            qk = _dot_nt(q_ref[rs, ks], k_ref[rs, ks])
            k_t = k_ref[rs, ks].astype(F32).T
            pre[sub] = (fc_r, ic_r, fc_b, d_mat, d_max, qk, k_t)
            yield
        for sub in ((0, 1) if fwd else (1, 0)):
            rs = slice(sub * SUB, (sub + 1) * SUB)
            fc_r, ic_r, fc_b, d_mat, d_max, qk, k_t = pre[sub]
            vh = v_ref[rs, vs]
            c_old = c_ref[...]
            n_old = nb_ref[...]
            m_old = m_ref[...]
            inter = fc_b + m_old
            m_i = jnp.maximum(inter, d_max)
            w_inter = jnp.exp(inter - m_i)
            s_mat = qk * jnp.exp(d_mat - m_i)
            lhs = jnp.concatenate([s_mat.astype(BF16),
                                   (w_inter * q_ref[rs, ks].astype(F32)).astype(BF16)], axis=1)
            num = _dot(lhs, jnp.concatenate([vh, c_old.astype(BF16)], axis=0))
            den = _dot(lhs, jnp.concatenate([ones, n_old.astype(BF16)], axis=0))
            f_last = fc_r[:, SUB - 1:SUB] if fwd else fc_r[:, 0:1]
            g_r = f_last - fc_r + ic_r
            m_new = jnp.maximum(f_last + m_old, jnp.max(g_r, axis=-1, keepdims=True))
            dec = jnp.exp(f_last + m_old - m_new)[:, 0:1]
            kw_t = (k_t * jnp.exp(g_r - m_new)).astype(BF16)
            c_upd = _dot(kw_t, vh)
            n_upd = _dot(kw_t, ones)
            yield
            r = 1.0 / jnp.maximum(jnp.abs(den), jnp.exp(-m_i))
            o_ref[rs, vs] = (num * jnp.concatenate([r, r], axis=1)).astype(o_ref.dtype)
            c_ref[...] = dec * c_old + c_upd
            nb_ref[...] = dec * n_old + n_upd
            m_ref[...] = m_new
            yield

    _staggered_round_robin([[unit(d, h) for h in hs for d in (0, 1)] for hs in ((0, 1), (2, 3))], lag=2)


def _mlstm_scan(q, k, v, gt):
    bsz, t_tot, _ = v.shape
    nblk = t_tot // BLK
    fwd, bwd = _scan_maps(nblk)
    n_g = 4 * N_HEADS

    def specs(m):
        return [pl.BlockSpec((None, BLK, QK_W), m), pl.BlockSpec((None, BLK, QK_W), m),
                pl.BlockSpec((None, BLK, V_W), m),
                pl.BlockSpec((None, n_g, BLK), lambda b, s: (b, 0, m(b, s)[1]))]

    return pl.pallas_call(
        _mlstm_scan_kernel,
        grid=(bsz, nblk),
        in_specs=specs(fwd) + specs(bwd),
        out_specs=[pl.BlockSpec((None, BLK, V_W), fwd), pl.BlockSpec((None, BLK, V_W), bwd)],
        out_shape=[jax.ShapeDtypeStruct((bsz, t_tot, V_W), BF16)] * 2,
        scratch_shapes=([pltpu.VMEM((D_K, D_V), F32)] * (2 * N_HEADS)
                        + [pltpu.VMEM((D_K, D_K), F32)] * (2 * N_HEADS)
                        + [pltpu.VMEM((1, D_K), F32)] * (2 * N_HEADS)),
        compiler_params=_params(("parallel", "arbitrary")),
        name="mlstm_scan",
    )(q, k, v, gt, q, k, v, gt)


OUT_PIECES = 2


def _head_norm_gate(of_ref, ob_ref, gate, ng_ref, rows):
    o = of_ref[rows, :].astype(F32) + ob_ref[rows, :].astype(F32)
    parts = []
    for h in range(N_HEADS):
        oh = o[:, h * D_V:(h + 1) * D_V]
        ms = jnp.mean(oh * oh, axis=-1, keepdims=True)
        parts.append(oh * lax.rsqrt(ms + EPS))
    return (jnp.concatenate(parts, axis=1) * ng_ref[...] * gate).astype(BF16)


def _mixer_out_pieces(gated, w_ref, finish):
    pr = BLK // OUT_PIECES
    rows = [slice(j * pr, (j + 1) * pr) for j in range(OUT_PIECES)]
    nxt = gated(rows[0])
    for j in range(OUT_PIECES):
        y = _dot(nxt, w_ref[...])
        if j + 1 < OUT_PIECES:
            nxt = gated(rows[j + 1])
        finish(rows[j], y)


def _gla_out_kernel(of_ref, ob_ref, r_ref, ctx_ref, g1_ref, ng_ref, w_ref, ctx_out, delta_out):
    t = pl.program_id(1)

    def gated(rows):
        r = r_ref[rows, :].astype(F32)
        return _head_norm_gate(of_ref, ob_ref, r * _sigmoid(r), ng_ref, rows)

    ys = []

    def finish(rows, y):
        delta_out[rows, :] = (g1_ref[...] * y).astype(delta_out.dtype)
        ys.append((rows, y))

    _mixer_out_pieces(gated, w_ref, finish)

    @pl.when(t == 0)
    def _():
        for rows, y in ys:
            ctx_out[rows, :] = ctx_ref[rows, :] + g1_ref[...] * y


def _gla_out(o_f, o_b, r, ctx, g1, ng, w_out):
    bsz, t_tot, _ = o_f.shape
    d = ctx.shape[2]
    nblk = t_tot // BLK
    tok = pl.BlockSpec((None, BLK, V_W), lambda b, t: (b, t, 0))
    ctx_spec = pl.BlockSpec((None, BLK, d), lambda b, t: (b, 0, 0))
    lat_spec = pl.BlockSpec((None, BLK, d), lambda b, t: (b, jnp.maximum(t - 1, 0), 0))
    return pl.pallas_call(
        _gla_out_kernel,
        grid=(bsz, nblk),
        in_specs=[tok, tok, tok, ctx_spec, _sel_spec(), _full((1, V_W)), _full((V_W, d))],
        out_specs=[ctx_spec, lat_spec],
        out_shape=[jax.ShapeDtypeStruct(ctx.shape, F32),
                   jax.ShapeDtypeStruct((bsz, t_tot - BLK, d), BF16)],
        compiler_params=_params(("parallel", "arbitrary")),
        name="gla_out",
    )(o_f, o_b, r, ctx, g1, ng, w_out)


def _mlstm_out_kernel(of_ref, ob_ref, og_ref, g1_ref, ng_ref, w_ref, delta_out):
    def gated(rows):
        return _head_norm_gate(of_ref, ob_ref, _sigmoid(og_ref[rows, :].astype(F32)), ng_ref, rows)

    def finish(rows, y):
        delta_out[rows, :] = (g1_ref[...] * y).astype(delta_out.dtype)

    _mixer_out_pieces(gated, w_ref, finish)


def _mlstm_out(h_f, h_b, og, g1, ng, w_out):
    bsz, t_tot, _ = h_f.shape
    d = w_out.shape[1]
    seq = t_tot - BLK
    rows = seq // GRID_W
    tok = pl.BlockSpec((None, BLK, V_W), lambda b, c: (b, c + 1, 0))
    out = pl.pallas_call(
        _mlstm_out_kernel,
        grid=(bsz, GRID_W),
        in_specs=[tok, tok, tok,
                  pl.BlockSpec((None, 1, d), lambda b, c: (b, 0, 0)),
                  _full((1, V_W)), _full((V_W, d))],
        out_specs=pl.BlockSpec((None, rows, d), lambda b, c: (b, 0, c)),
        out_shape=jax.ShapeDtypeStruct((bsz, rows, GRID_W * d), BF16),
        compiler_params=_params(("parallel", "arbitrary")),
        name="mlstm_out",
    )(h_f, h_b, og, g1, ng, w_out)
    return out.reshape(bsz, seq, d)


GELU_K0 = 0.7978845608028654
GELU_K1 = GELU_K0 * 0.044715


def _ffn_kernel(*refs, width, tm, vertical, has_delta, final_norm):
    refs = list(refs)
    x_ref = refs.pop(0)
    xu_ref, xd_ref = (refs.pop(0), refs.pop(0)) if vertical else (None, None)
    d_ref, du_ref, dd_ref = (refs.pop(0), refs.pop(0), refs.pop(0)) if has_delta else (None,) * 3
    sc_ref, sh_ref, g2_ref, ng_ref, wa_ref, wg_ref, cw_ref, cb_ref, wd_ref = refs[:9]
    refs = refs[9:]
    fg_ref = refs.pop(0) if final_norm else None
    out_ref, hext_ref, act_ref = refs

    def residual(x_r, delta_r):
        return x_r[...] + delta_r[...].astype(F32) if has_delta else x_r[...]

    t = pl.program_id(1)
    nt = pl.num_programs(1)
    halo = width if vertical else 0
    ext = tm + 2 * halo

    def hmod(xt):
        return _rms_mod(xt, ng_ref[...], sc_ref[...], sh_ref[...]).astype(BF16)

    assert width & (width - 1) == 0
    pr = max(width, min(tm, FFN_PIECE_ROWS))
    assert pr % width == 0 and tm % pr == 0
    n_pieces = tm // pr
    wpos = lax.broadcasted_iota(jnp.int32, (pr, FF_CHUNK), 0) & (width - 1)
    first_col = wpos == 0
    last_col = wpos == width - 1
    taps = (0, 1, 2) if vertical else (1,)

    def chunk(c):
        return slice(c * FF_CHUNK, (c + 1) * FF_CHUNK)

    def activation(c, a, g, r0):
        cw = cw_ref[:, chunk(c)]
        u = []
        for dw in range(3):
            s = None
            for dr in taps:
                term = g[dr * halo + r0:dr * halo + r0 + pr] * cw[3 * dr + dw:3 * dr + dw + 1]
                s = term if s is None else s + term
            u.append(s)
        conv = ((u[1] + cb_ref[:, chunk(c)]) + jnp.where(first_col, 0.0, pltpu.roll(u[0], 1, 0))
                + jnp.where(last_col, 0.0, pltpu.roll(u[2], pr - 1, 0)))
        th = jnp.tanh(conv * (GELU_K0 + GELU_K1 * (conv * conv)))
        return ((conv + conv * th) * a[r0:r0 + pr]).astype(BF16)

    def up_a(c):
        return _dot(hext_ref[halo:halo + tm, :], wa_ref[:, chunk(c)])

    def up_g(c):
        return _dot(hext_ref[...], wg_ref[:, chunk(c)])

    out_ref[...] = residual(x_ref, d_ref)
    hext_ref[halo:halo + tm, :] = hmod(out_ref[...])
    if vertical:
        zeros = jnp.zeros((halo, hext_ref.shape[1]), BF16)
        hext_ref[0:halo, :] = jnp.where(t == 0, zeros, hmod(residual(xu_ref, du_ref)))
        hext_ref[halo + tm:ext, :] = jnp.where(t == nt - 1, zeros, hmod(residual(xd_ref, dd_ref)))

    state = {"a": up_a(0), "g": up_g(0)}
    for c in range(N_FF_CHUNKS):
        a, g = state["a"], state["g"]
        matmuls = []
        if c + 1 < N_FF_CHUNKS:
            matmuls.append(lambda c=c: state.__setitem__("a", up_a(c + 1)))
            matmuls.append(lambda c=c: state.__setitem__("g", up_g(c + 1)))
        for j in range(n_pieces):
            if matmuls:
                matmuls.pop(0)()
            act_ref[j * pr:(j + 1) * pr, chunk(c)] = activation(c, a, g, j * pr)
        for m in matmuls:
            m()
    y = out_ref[...] + g2_ref[...] * _dot(act_ref[...], wd_ref[...])
    if final_norm:
        ms = jnp.mean(y * y, axis=-1, keepdims=True)
        y = y * lax.rsqrt(ms + EPS) * fg_ref[...]
    out_ref[...] = y


def _conv_ffn(x, delta, sc, sh, g2, ng, wa, wg, cw, cb, wd, *, width, tm, vertical, final_g=None):
    bsz, seq, d = x.shape
    nt = seq // tm
    vec = pl.BlockSpec((None, 1, d), lambda b, t: (b, 0, 0))
    tile = pl.BlockSpec((None, tm, d), lambda b, t: (b, t, 0))
    halo_specs = []
    if vertical:
        rpt = tm // width
        nrows = seq // width
        halo_specs = [pl.BlockSpec((None, width, d), lambda b, t: (b, jnp.maximum(t * rpt - 1, 0), 0)),
                      pl.BlockSpec((None, width, d), lambda b, t: (b, jnp.minimum((t + 1) * rpt, nrows - 1), 0))]
    in_specs = [tile] + halo_specs
    args = [x] * (1 + len(halo_specs))
    if delta is not None:
        assert vertical
        in_specs += [tile] + halo_specs
        args += [delta] * 3
    in_specs += [vec, vec, vec, _full((1, d)), _resident(wa.shape), _resident(wg.shape),
                 _resident(cw.shape), _resident(cb.shape), _resident(wd.shape)]
    args += [sc, sh, g2, ng, wa, wg, cw, cb, wd]
    if final_g is not None:
        in_specs.append(_full((1, d)))
        args.append(final_g)
    ext = tm + (2 * width if vertical else 0)
    return pl.pallas_call(
        functools.partial(_ffn_kernel, width=width, tm=tm, vertical=vertical,
                          has_delta=delta is not None, final_norm=final_g is not None),
        grid=(bsz, nt),
        in_specs=in_specs,
        out_specs=tile,
        out_shape=jax.ShapeDtypeStruct(x.shape, F32),
        scratch_shapes=[pltpu.VMEM((ext, d), BF16), pltpu.VMEM((tm, wd.shape[0]), BF16)],
        compiler_params=_params(("parallel", "arbitrary")),
        name="conv_ffn_grid" if vertical else "conv_ffn_seq",
    )(*args)


def _ffn_weights(w_up, conv_w, conv_b, w_down):
    d = w_up.shape[0]

    assert w_up.shape == (d, 2 * D_FF)
    wa = w_up[:, :D_FF].astype(BF16)
    wg = w_up[:, D_FF:].astype(BF16)
    cw = conv_w.reshape(9, D_FF)
    cb = conv_b.reshape(1, D_FF)
    wd = (0.5 * w_down).astype(BF16)
    return wa, wg, cw, cb, wd


def kernel(x, c, ctx, c_ctx, ada_w, ada_b, norm_mix_g, norm_ffn_g, gla_w_in, gla_w_a2, gla_b_a2, gla_norm_g, gla_w_out, mlstm_w_in, mlstm_b_gate, mlstm_conv_w, mlstm_conv_b, mlstm_norm_g, mlstm_w_out, ffn_w_up, ffn_conv_w, ffn_conv_b, ffn_w_down, final_norm_g):
    bsz, seq, d = x.shape
    depth = ada_w.shape[0]
    assert bsz == 2 and d == D_MODEL and ctx.shape[1] == BLK and seq == BLK * GRID_W and depth == 2

    cvec = jnp.zeros((8, d), F32).at[:bsz].set(c).at[bsz].set(c_ctx)
    mods = _modulation(cvec, ada_w, ada_b)

    def mod(i, j):
        return mods[i, :3, j * d:(j + 1) * d].reshape(3, 1, d)

    def row(v):
        return v.reshape(1, -1)

    sh1, sc1, g1, sh2, sc2, g2 = (mod(0, j) for j in range(N_MOD))
    nsh1, nsc1, ng1, nsh2, nsc2, ng2 = (mod(1, j) for j in range(N_MOD))
    q, k, v, r, za = _gla_in(ctx, x, sc1, sh1, row(norm_mix_g[0]), gla_w_in[0].astype(BF16))
    o_f, o_b = _gla_scan(q, k, v, za, gla_w_a2[0], gla_b_a2[0])
    ctx, delta = _gla_out(o_f, o_b, r, ctx, g1, row(gla_norm_g[0]), gla_w_out[0].astype(BF16))
    ffn_w = _ffn_weights(ffn_w_up[0], ffn_conv_w[0], ffn_conv_b[0], ffn_w_down[0])
    x = _conv_ffn(x, delta, sc2[:bsz], sh2[:bsz], g2[:bsz], row(norm_ffn_g[0]), *ffn_w,
                  width=GRID_W, tm=FFN_ROWS * GRID_W, vertical=True)
    ctx = _conv_ffn(ctx, None, jnp.broadcast_to(sc2[2:], (bsz, 1, d)), jnp.broadcast_to(sh2[2:], (bsz, 1, d)),
                    jnp.broadcast_to(g2[2:], (bsz, 1, d)), row(norm_ffn_g[0]), *ffn_w,
                    width=BLK, tm=BLK, vertical=False)

    q, k, v, og, gt = _mlstm_in(ctx, x, nsc1, nsh1, row(norm_mix_g[1]), mlstm_w_in[0].astype(BF16),
                                mlstm_b_gate[0], mlstm_conv_w[0], mlstm_conv_b[0])
    h_f, h_b = _mlstm_scan(q, k, v, gt)
    delta = _mlstm_out(h_f, h_b, og, ng1[:bsz], row(mlstm_norm_g[0]), mlstm_w_out[0].astype(BF16))
    ffn_w = _ffn_weights(ffn_w_up[1], ffn_conv_w[1], ffn_conv_b[1], ffn_w_down[1])
    return _conv_ffn(x, delta, nsc2[:bsz], nsh2[:bsz], ng2[:bsz], row(norm_ffn_g[1]), *ffn_w,
                     width=GRID_W, tm=FFN_ROWS * GRID_W, vertical=True, final_g=row(final_norm_g))
```

```python
import functools

import jax
import jax.numpy as jnp
from jax import lax
from jax.experimental import pallas as pl
from jax.experimental.pallas import tpu as pltpu

D_MODEL = 1024
GRID_W = 64
N_HEADS = 4
D_K = 128
D_V = 256
QK_W = N_HEADS * D_K
V_W = N_HEADS * D_V
GLA_RANK = 16
GLA_TAU = 16.0
D_FF = 2816
N_MOD = 6
EPS = 1e-6

BLK = 256
SUB = 128
FF_CHUNK = 256
N_FF_CHUNKS = D_FF // FF_CHUNK
FFN_ROWS = 16
FFN_PIECE_ROWS = 256
VMEM_LIMIT_BYTES = 62 * 1024 * 1024

F32 = jnp.float32
BF16 = jnp.bfloat16


def _dot(a, b):
    return jnp.dot(a, b, preferred_element_type=F32)


def _dot_nt(a, b):
    return lax.dot_general(a, b, (((1,), (1,)), ((), ())), preferred_element_type=F32)


def _sigmoid(x):
    return 1.0 / (1.0 + jnp.exp(-x))


def _log_sigmoid(x):
    return jnp.minimum(x, 0.0) - jnp.log(1.0 + jnp.exp(-jnp.abs(x)))


def _rms_mod(x, ng, sc, sh):
    ms = jnp.mean(x * x, axis=-1, keepdims=True)
    return (x * lax.rsqrt(ms + EPS) * ng) * (1.0 + sc) + sh


def _split_bf16(x):
    hi = x.astype(BF16)
    lo = (x - hi.astype(F32)).astype(BF16)
    return hi, lo


def _params(sem):
    return pltpu.CompilerParams(dimension_semantics=sem, vmem_limit_bytes=VMEM_LIMIT_BYTES)


def _full(shape):
    n = len(shape)
    return pl.BlockSpec(shape, lambda *_: (0,) * n)


def _resident(shape):
    n = len(shape)
    return pl.BlockSpec(shape, lambda *_: (0,) * n, pipeline_mode=pl.Buffered(1))


def _sel_spec():
    return pl.BlockSpec((None, 1, D_MODEL), lambda b, t: (jnp.where(t == 0, 2, b), 0, 0))


def _mod_kernel(c_ref, w_ref, b_ref, o_ref):
    c = c_ref[...]
    s = c * _sigmoid(c)
    o_ref[...] = jnp.dot(s, w_ref[...], preferred_element_type=F32,
                         precision=lax.Precision.HIGHEST) + b_ref[...]


def _modulation(cvec, ada_w, ada_b):
    depth = ada_w.shape[0]
    n = N_MOD * D_MODEL
    tn = 1536
    return pl.pallas_call(
        _mod_kernel,
        grid=(depth, n // tn),
        in_specs=[pl.BlockSpec((8, D_MODEL), lambda i, j: (0, 0)),
                  pl.BlockSpec((None, D_MODEL, tn), lambda i, j: (i, 0, j)),
                  pl.BlockSpec((None, 1, tn), lambda i, j: (i, 0, j))],
        out_specs=pl.BlockSpec((None, 8, tn), lambda i, j: (i, 0, j)),
        out_shape=jax.ShapeDtypeStruct((depth, 8, n), F32),
        compiler_params=_params(("parallel", "parallel")),
        name="modulation",
    )(cvec, ada_w, ada_b.reshape(depth, 1, n))


def _load_tile(ctx_ref, x_ref):
    t = pl.program_id(1)
    return jnp.where(t == 0, ctx_ref[...], x_ref[...])


def _gla_in_kernel(ctx_ref, x_ref, sc_ref, sh_ref, ng_ref, w_ref,
                   q_ref, k_ref, v_ref, r_ref, za_ref):
    h = _rms_mod(_load_tile(ctx_ref, x_ref), ng_ref[...], sc_ref[...], sh_ref[...]).astype(BF16)
    q_ref[...] = (_dot(h, w_ref[:, 0:QK_W]) * D_K ** -0.5).astype(BF16)
    k_ref[...] = _dot(h, w_ref[:, QK_W:2 * QK_W]).astype(BF16)
    v_ref[...] = _dot(h, w_ref[:, 2 * QK_W:2 * QK_W + V_W]).astype(BF16)
    r_ref[...] = _dot(h, w_ref[:, 2 * QK_W + V_W:2 * QK_W + 2 * V_W]).astype(BF16)
    za_ref[...] = _dot(h, w_ref[:, 2 * QK_W + 2 * V_W:])


def _gla_in(ctx, x, sc, sh, ng, w_in):
    bsz, seq, d = x.shape
    nblk = 1 + seq // BLK
    t_tot = nblk * BLK
    n_in = w_in.shape[1]

    def tok(width):
        return pl.BlockSpec((None, BLK, width), lambda b, t: (b, t, 0))

    return pl.pallas_call(
        _gla_in_kernel,
        grid=(bsz, nblk),
        in_specs=[pl.BlockSpec((None, BLK, d), lambda b, t: (b, 0, 0)),
                  pl.BlockSpec((None, BLK, d), lambda b, t: (b, jnp.maximum(t - 1, 0), 0)),
                  _sel_spec(), _sel_spec(), _full((1, d)), _full((d, n_in))],
        out_specs=[tok(QK_W), tok(QK_W), tok(V_W), tok(V_W), tok(2 * GLA_RANK)],
        out_shape=[jax.ShapeDtypeStruct((bsz, t_tot, QK_W), BF16),
                   jax.ShapeDtypeStruct((bsz, t_tot, QK_W), BF16),
                   jax.ShapeDtypeStruct((bsz, t_tot, V_W), BF16),
                   jax.ShapeDtypeStruct((bsz, t_tot, V_W), BF16),
                   jax.ShapeDtypeStruct((bsz, t_tot, 2 * GLA_RANK), F32)],
        compiler_params=_params(("parallel", "arbitrary")),
        name="gla_in",
    )(ctx, x, sc, sh, ng, w_in)


def _mlstm_in_kernel(ctx_ref, x_ref, sc_ref, sh_ref, ng_ref, w_ref, wgt_ref, bgt_ref, cw_ref, cb_ref,
                     q_ref, k_ref, v_ref, og_ref, gt_ref, pre_ref, last_ref, *, nblk):
    t = pl.program_id(1)

    @pl.when(t == 0)
    def _():
        last_ref[...] = jnp.zeros_like(last_ref)
        pre_ref[...] = jnp.zeros_like(pre_ref)

    h = _rms_mod(_load_tile(ctx_ref, x_ref), ng_ref[...], sc_ref[...], sh_ref[...]).astype(BF16)

    blk = t - 1
    x = pre_ref[(t + 1) % 2]
    pre = _dot(h, w_ref[:, 0:2 * QK_W])
    pre_ref[t % 2] = pre
    has_prev = blk > 1
    has_next = jnp.logical_and(blk > 0, blk != nblk - 1)
    prev = jnp.where(has_prev, last_ref[0:1, :], 0.0)
    nxt = jnp.where(has_next, pre[0:1, :], 0.0)
    row = lax.broadcasted_iota(jnp.int32, (BLK, 1), 0)
    x_m1 = jnp.where(row == 0, prev, pltpu.roll(x, 1, 0))
    x_p1 = jnp.where(row == BLK - 1, nxt, pltpu.roll(x, BLK - 1, 0))
    last_ref[0:1, :] = x[BLK - 1:BLK, :]

    def conv_piece(r0, r1):
        qk = (x_m1[r0:r1] * cw_ref[0:1, :] + x[r0:r1] * cw_ref[1:2, :] + x_p1[r0:r1] * cw_ref[2:3, :]
              + cb_ref[...])
        qk = qk * _sigmoid(qk)
        q_ref[r0:r1, :] = qk[:, :QK_W].astype(BF16)
        k_ref[r0:r1, :] = (qk[:, QK_W:] * D_K ** -0.5).astype(BF16)

    def gates():
        gt = _dot_nt(wgt_ref[...], h) + bgt_ref[...]
        is_forget = lax.broadcasted_iota(jnp.int32, gt.shape, 0) >= 2 * N_HEADS
        gt_ref[...] = jnp.where(is_forget, _log_sigmoid(gt), gt)

    n_pieces = 4
    step = BLK // n_pieces
    conv_piece(0, step)
    v_ref[...] = _dot(h, w_ref[:, 2 * QK_W:2 * QK_W + V_W]).astype(BF16)
    conv_piece(step, 2 * step)
    og_ref[...] = _dot(h, w_ref[:, 2 * QK_W + V_W:2 * QK_W + 2 * V_W]).astype(BF16)
    conv_piece(2 * step, 3 * step)
    gates()
    conv_piece(3 * step, BLK)


def _mlstm_in(ctx, x, sc, sh, ng, w_in, b_gate, conv_w, conv_b):
    bsz, seq, d = x.shape
    rows = seq // GRID_W
    assert rows == BLK
    nblk = 1 + GRID_W
    t_tot = nblk * BLK
    n_in = w_in.shape[1]
    n_g = 4 * N_HEADS
    xcol = x.reshape(bsz, rows, GRID_W * d)
    w_gt = w_in[:, 2 * QK_W + 2 * V_W:].T

    def late(width):
        return pl.BlockSpec((None, BLK, width), lambda b, t: (b, jnp.maximum(t - 1, 0), 0))

    def tok(width):
        return pl.BlockSpec((None, BLK, width), lambda b, t: (b, jnp.minimum(t, nblk - 1), 0))

    return pl.pallas_call(
        functools.partial(_mlstm_in_kernel, nblk=nblk),
        grid=(bsz, nblk + 1),
        in_specs=[pl.BlockSpec((None, BLK, d), lambda b, t: (b, 0, 0)),
                  pl.BlockSpec((None, rows, d), lambda b, t: (b, 0, jnp.clip(t - 1, 0, GRID_W - 1))),
                  _sel_spec(), _sel_spec(), _full((1, d)), _full((d, n_in)), _full((n_g, d)),
                  _full((n_g, 1)), _full((3, 2 * QK_W)), _full((1, 2 * QK_W))],
        out_specs=[late(QK_W), late(QK_W), tok(V_W), tok(V_W),
                   pl.BlockSpec((None, n_g, BLK), lambda b, t: (b, 0, jnp.minimum(t, nblk - 1)))],
        out_shape=[jax.ShapeDtypeStruct((bsz, t_tot, QK_W), BF16),
                   jax.ShapeDtypeStruct((bsz, t_tot, QK_W), BF16),
                   jax.ShapeDtypeStruct((bsz, t_tot, V_W), BF16),
                   jax.ShapeDtypeStruct((bsz, t_tot, V_W), BF16),
                   jax.ShapeDtypeStruct((bsz, n_g, t_tot), F32)],
        scratch_shapes=[pltpu.VMEM((2, BLK, 2 * QK_W), F32), pltpu.VMEM((8, 2 * QK_W), F32)],
        compiler_params=_params(("parallel", "arbitrary")),
        name="mlstm_in",
    )(ctx, xcol, sc, sh, ng, w_in, w_gt, b_gate.reshape(n_g, 1), conv_w, conv_b.reshape(1, 2 * QK_W))


def _block_tri(n, sub, lower):
    ri = lax.broadcasted_iota(jnp.int32, (n, n), 0)
    ci = lax.broadcasted_iota(jnp.int32, (n, n), 1)
    shift = sub.bit_length() - 1
    assert sub == 1 << shift
    same = (ri >> shift) == (ci >> shift)
    tri = (ci <= ri) if lower else (ci >= ri)
    return jnp.where(same & tri, 1.0, 0.0).astype(BF16)


def _tri_mask(n, lower):
    ri = lax.broadcasted_iota(jnp.int32, (n, n), 0)
    ci = lax.broadcasted_iota(jnp.int32, (n, n), 1)
    return (ci <= ri) if lower else (ci >= ri)


def _staggered_round_robin(groups, lag):
    live, rnd = [], 0
    while live or rnd <= lag * (len(groups) - 1):
        if rnd % lag == 0 and rnd // lag < len(groups):
            live = live + list(groups[rnd // lag])
        alive = []
        for gen in live:
            try:
                next(gen)
                alive.append(gen)
            except StopIteration:
                pass
        live, rnd = alive, rnd + 1


def _gla_scan_kernel(qf, kf, vf, zf, qb, kb, vb, zb, w2_ref, b2_ref, of_ref, ob_ref, *s_refs):
    @pl.when(pl.program_id(1) == 0)
    def _():
        for s_ref in s_refs:
            s_ref[...] = jnp.zeros_like(s_ref)

    dirs = ((qf, kf, vf, zf, of_ref), (qb, kb, vb, zb, ob_ref))
    tris = (_block_tri(BLK, SUB, lower=True), _block_tri(BLK, SUB, lower=False))
    masks = (_tri_mask(SUB, lower=True), _tri_mask(SUB, lower=False))

    def unit(d, h):
        q_ref, k_ref, v_ref, z_ref, o_ref = dirs[d]
        s_ref = s_refs[d * N_HEADS + h]
        fwd = d == 0
        ks = slice(h * D_K, (h + 1) * D_K)
        vs = slice(h * D_V, (h + 1) * D_V)
        za = z_ref[:, d * GLA_RANK:(d + 1) * GLA_RANK].astype(BF16)
        y = _dot(za, w2_ref[d, :, ks])
        yield
        la = _log_sigmoid(y + b2_ref[d, :, ks]) * (1.0 / GLA_TAU)
        hi, lo = _split_bf16(la)
        bcum = _dot(tris[d], jnp.concatenate([hi, lo], axis=1))
        yield
        bcum = bcum[:, :D_K] + bcum[:, D_K:]
        ops = {}
        for sub in (0, 1):
            rs = slice(sub * SUB, (sub + 1) * SUB)
            bs = bcum[rs]
            tot = bs[SUB - 1:SUB] if fwd else bs[0:1]
            mid = 0.5 * tot
            e_mid = jnp.exp(mid)
            q_mid = q_ref[rs, ks].astype(F32) * jnp.exp(bs - mid)
            k_mid = k_ref[rs, ks].astype(F32) * jnp.exp(mid - bs)
            q_in = (q_mid * e_mid).astype(BF16)
            k_st_t = (k_mid * e_mid).T.astype(BF16)
            ops[sub] = (q_mid.astype(BF16), q_in, k_mid.astype(BF16), k_st_t, e_mid * e_mid)
            yield
        for sub in ((0, 1) if fwd else (1, 0)):
            rs = slice(sub * SUB, (sub + 1) * SUB)
            q_mid, q_in, k_mid, k_st_t, decay = ops[sub]
            scores = _dot_nt(q_mid, k_mid)
            yield
            p = jnp.where(masks[d], scores, 0.0).astype(BF16)
            vh = v_ref[rs, vs]
            s_old = s_ref[...]
            o = _dot(jnp.concatenate([p, q_in], axis=1), jnp.concatenate([vh, s_old.astype(BF16)], axis=0))
            upd = _dot(k_st_t, vh)
            yield
            o_ref[rs, vs] = o.astype(o_ref.dtype)
            dcol = jnp.broadcast_to(decay, (D_K, D_K)).T
            s_ref[...] = s_old * jnp.concatenate([dcol, dcol], axis=1) + upd
            yield

    _staggered_round_robin([[unit(d, h) for h in hs for d in (0, 1)] for hs in ((0, 1), (2, 3))], lag=3)


def _scan_maps(nblk):
    def fwd(b, s):
        return (b, s, 0)

    def bwd(b, s):
        return (b, jnp.where(s == 0, 0, nblk - s), 0)

    return fwd, bwd


def _gla_scan(q, k, v, za, w_a2, b_a2):
    bsz, t_tot, _ = q.shape
    nblk = t_tot // BLK
    fwd, bwd = _scan_maps(nblk)

    def specs(m):
        return [pl.BlockSpec((None, BLK, QK_W), m), pl.BlockSpec((None, BLK, QK_W), m),
                pl.BlockSpec((None, BLK, V_W), m), pl.BlockSpec((None, BLK, 2 * GLA_RANK), m)]

    return pl.pallas_call(
        _gla_scan_kernel,
        grid=(bsz, nblk),
        in_specs=specs(fwd) + specs(bwd) + [_full((2, GLA_RANK, QK_W)), _full((2, 1, QK_W))],
        out_specs=[pl.BlockSpec((None, BLK, V_W), fwd), pl.BlockSpec((None, BLK, V_W), bwd)],
        out_shape=[jax.ShapeDtypeStruct((bsz, t_tot, V_W), BF16)] * 2,
        scratch_shapes=[pltpu.VMEM((D_K, D_V), F32)] * (2 * N_HEADS),
        compiler_params=_params(("parallel", "arbitrary")),
        name="gla_scan",
    )(q, k, v, za, q, k, v, za, w_a2.astype(BF16), b_a2.reshape(2, 1, QK_W))


def _mlstm_scan_kernel(qf, kf, vf, gtf, qb, kb, vb, gtb, of_ref, ob_ref, *state_refs):
    n_units = 2 * N_HEADS
    c_refs, nb_refs, m_refs = (state_refs[i * n_units:(i + 1) * n_units] for i in range(3))

    @pl.when(pl.program_id(1) == 0)
    def _():
        for ref in state_refs:
            ref[...] = jnp.zeros_like(ref)

    ones = jnp.ones((SUB, D_K), BF16)
    dirs = ((qf, kf, vf, gtf, of_ref), (qb, kb, vb, gtb, ob_ref))
    gates = []
    for d, (_, _, _, gt_ref, _) in enumerate(dirs):
        gt = gt_ref[...]
        gt_hi, gt_lo = _split_bf16(gt)
        tri_r = _block_tri(BLK, SUB, lower=d != 0)
        gates.append((gt, _dot(gt_hi, tri_r) + _dot(gt_lo, tri_r)))
    masks = (_tri_mask(SUB, lower=True), _tri_mask(SUB, lower=False))
    def unit(d, h):
        q_ref, k_ref, v_ref, _, o_ref = dirs[d]
        fwd = d == 0
        gt, f_row = gates[d]
        ks = slice(h * D_K, (h + 1) * D_K)
        vs = slice(h * D_V, (h + 1) * D_V)
        gi = d * N_HEADS + h
        gf_ = 2 * N_HEADS + gi
        c_ref, nb_ref, m_ref = c_refs[gi], nb_refs[gi], m_refs[gi]
        pre = {}
        for sub in (0, 1):
            rs = slice(sub * SUB, (sub + 1) * SUB)
            fc_r = f_row[gf_:gf_ + 1, rs]
            ic_r = gt[gi:gi + 1, rs]
            fc_b = jnp.broadcast_to(fc_r, (SUB, SUB)).T
            d_mat = jnp.where(masks[d], fc_b + (ic_r - fc_r), -jnp.inf)
            d_max = jnp.max(d_mat, axis=-1, keepdims=True)
            qk = _dot_nt(q_ref[rs, ks], k_ref[rs, ks])
            k_t = k_ref[rs, ks].astype(F32).T
            pre[sub] = (fc_r, ic_r, fc_b, d_mat, d_max, qk, k_t)
            yield
        for sub in ((0, 1) if fwd else (1, 0)):
            rs = slice(sub * SUB, (sub + 1) * SUB)
            fc_r, ic_r, fc_b, d_mat, d_max, qk, k_t = pre[sub]
            vh = v_ref[rs, vs]
            c_old = c_ref[...]
            n_old = nb_ref[...]
            m_old = m_ref[...]
            inter = fc_b + m_old
            m_i = jnp.maximum(inter, d_max)
            w_inter = jnp.exp(inter - m_i)
            s_mat = qk * jnp.exp(d_mat - m_i)
            lhs = jnp.concatenate([s_mat.astype(BF16),
                                   (w_inter * q_ref[rs, ks].astype(F32)).astype(BF16)], axis=1)
            num = _dot(lhs, jnp.concatenate([vh, c_old.astype(BF16)], axis=0))
            den = _dot(lhs, jnp.concatenate([ones, n_old.astype(BF16)], axis=0))
            f_last = fc_r[:, SUB - 1:SUB] if fwd else fc_r[:, 0:1]
            g_r = f_last - fc_r + ic_r
            m_new = jnp.maximum(f_last + m_old, jnp.max(g_r, axis=-1, keepdims=True))
            dec = jnp.exp(f_last + m_old - m_new)[:, 0:1]
            kw_t = (k_t * jnp.exp(g_r - m_new)).astype(BF16)
            c_upd = _dot(kw_t, vh)
            n_upd = _dot(kw_t, ones)
            yield
            r = 1.0 / jnp.maximum(jnp.abs(den), jnp.exp(-m_i))
            o_ref[rs, vs] = (num * jnp.concatenate([r, r], axis=1)).astype(o_ref.dtype)
            c_ref[...] = dec * c_old + c_upd
            nb_ref[...] = dec * n_old + n_upd
            m_ref[...] = m_new
            yield

    _staggered_round_robin([[unit(d, h) for h in hs for d in (0, 1)] for hs in ((0, 1), (2, 3))], lag=2)


def _mlstm_scan(q, k, v, gt):
    bsz, t_tot, _ = v.shape
    nblk = t_tot // BLK
    fwd, bwd = _scan_maps(nblk)
    n_g = 4 * N_HEADS

    def specs(m):
        return [pl.BlockSpec((None, BLK, QK_W), m), pl.BlockSpec((None, BLK, QK_W), m),
                pl.BlockSpec((None, BLK, V_W), m),
                pl.BlockSpec((None, n_g, BLK), lambda b, s: (b, 0, m(b, s)[1]))]

    return pl.pallas_call(
        _mlstm_scan_kernel,
        grid=(bsz, nblk),
        in_specs=specs(fwd) + specs(bwd),
        out_specs=[pl.BlockSpec((None, BLK, V_W), fwd), pl.BlockSpec((None, BLK, V_W), bwd)],
        out_shape=[jax.ShapeDtypeStruct((bsz, t_tot, V_W), BF16)] * 2,
        scratch_shapes=([pltpu.VMEM((D_K, D_V), F32)] * (2 * N_HEADS)
                        + [pltpu.VMEM((D_K, D_K), F32)] * (2 * N_HEADS)
                        + [pltpu.VMEM((1, D_K), F32)] * (2 * N_HEADS)),
        compiler_params=_params(("parallel", "arbitrary")),
        name="mlstm_scan",
    )(q, k, v, gt, q, k, v, gt)


OUT_PIECES = 2


def _head_norm_gate(of_ref, ob_ref, gate, ng_ref, rows):
    o = of_ref[rows, :].astype(F32) + ob_ref[rows, :].astype(F32)
    parts = []
    for h in range(N_HEADS):
        oh = o[:, h * D_V:(h + 1) * D_V]
        ms = jnp.mean(oh * oh, axis=-1, keepdims=True)
        parts.append(oh * lax.rsqrt(ms + EPS))
    return (jnp.concatenate(parts, axis=1) * ng_ref[...] * gate).astype(BF16)


def _mixer_out_pieces(gated, w_ref, finish):
    pr = BLK // OUT_PIECES
    rows = [slice(j * pr, (j + 1) * pr) for j in range(OUT_PIECES)]
    nxt = gated(rows[0])
    for j in range(OUT_PIECES):
        y = _dot(nxt, w_ref[...])
        if j + 1 < OUT_PIECES:
            nxt = gated(rows[j + 1])
        finish(rows[j], y)


def _gla_out_kernel(of_ref, ob_ref, r_ref, ctx_ref, g1_ref, ng_ref, w_ref, ctx_out, delta_out):
    t = pl.program_id(1)

    def gated(rows):
        r = r_ref[rows, :].astype(F32)
        return _head_norm_gate(of_ref, ob_ref, r * _sigmoid(r), ng_ref, rows)

    ys = []

    def finish(rows, y):
        delta_out[rows, :] = (g1_ref[...] * y).astype(delta_out.dtype)
        ys.append((rows, y))

    _mixer_out_pieces(gated, w_ref, finish)

    @pl.when(t == 0)
    def _():
        for rows, y in ys:
            ctx_out[rows, :] = ctx_ref[rows, :] + g1_ref[...] * y


def _gla_out(o_f, o_b, r, ctx, g1, ng, w_out):
    bsz, t_tot, _ = o_f.shape
    d = ctx.shape[2]
    nblk = t_tot // BLK
    tok = pl.BlockSpec((None, BLK, V_W), lambda b, t: (b, t, 0))
    ctx_spec = pl.BlockSpec((None, BLK, d), lambda b, t: (b, 0, 0))
    lat_spec = pl.BlockSpec((None, BLK, d), lambda b, t: (b, jnp.maximum(t - 1, 0), 0))
    return pl.pallas_call(
        _gla_out_kernel,
        grid=(bsz, nblk),
        in_specs=[tok, tok, tok, ctx_spec, _sel_spec(), _full((1, V_W)), _full((V_W, d))],
        out_specs=[ctx_spec, lat_spec],
        out_shape=[jax.ShapeDtypeStruct(ctx.shape, F32),
                   jax.ShapeDtypeStruct((bsz, t_tot - BLK, d), BF16)],
        compiler_params=_params(("parallel", "arbitrary")),
        name="gla_out",
    )(o_f, o_b, r, ctx, g1, ng, w_out)


def _mlstm_out_kernel(of_ref, ob_ref, og_ref, g1_ref, ng_ref, w_ref, delta_out):
    def gated(rows):
        return _head_norm_gate(of_ref, ob_ref, _sigmoid(og_ref[rows, :].astype(F32)), ng_ref, rows)

    def finish(rows, y):
        delta_out[rows, :] = (g1_ref[...] * y).astype(delta_out.dtype)

    _mixer_out_pieces(gated, w_ref, finish)


def _mlstm_out(h_f, h_b, og, g1, ng, w_out):
    bsz, t_tot, _ = h_f.shape
    d = w_out.shape[1]
    seq = t_tot - BLK
    rows = seq // GRID_W
    tok = pl.BlockSpec((None, BLK, V_W), lambda b, c: (b, c + 1, 0))
    out = pl.pallas_call(
        _mlstm_out_kernel,
        grid=(bsz, GRID_W),
        in_specs=[tok, tok, tok,
                  pl.BlockSpec((None, 1, d), lambda b, c: (b, 0, 0)),
                  _full((1, V_W)), _full((V_W, d))],
        out_specs=pl.BlockSpec((None, rows, d), lambda b, c: (b, 0, c)),
        out_shape=jax.ShapeDtypeStruct((bsz, rows, GRID_W * d), BF16),
        compiler_params=_params(("parallel", "arbitrary")),
        name="mlstm_out",
    )(h_f, h_b, og, g1, ng, w_out)
    return out.reshape(bsz, seq, d)


GELU_K0 = 0.7978845608028654
GELU_K1 = GELU_K0 * 0.044715


def _ffn_kernel(*refs, width, tm, vertical, has_delta, final_norm):
    refs = list(refs)
    x_ref = refs.pop(0)
    xu_ref, xd_ref = (refs.pop(0), refs.pop(0)) if vertical else (None, None)
    d_ref, du_ref, dd_ref = (refs.pop(0), refs.pop(0), refs.pop(0)) if has_delta else (None,) * 3
    sc_ref, sh_ref, g2_ref, ng_ref, wa_ref, wg_ref, cw_ref, cb_ref, wd_ref = refs[:9]
    refs = refs[9:]
    fg_ref = refs.pop(0) if final_norm else None
    out_ref, hext_ref, act_ref = refs

    def residual(x_r, delta_r):
        return x_r[...] + delta_r[...].astype(F32) if has_delta else x_r[...]

    t = pl.program_id(1)
    nt = pl.num_programs(1)
    halo = width if vertical else 0
    ext = tm + 2 * halo

    def hmod(xt):
        return _rms_mod(xt, ng_ref[...], sc_ref[...], sh_ref[...]).astype(BF16)

    assert width & (width - 1) == 0
    pr = max(width, min(tm, FFN_PIECE_ROWS))
    assert pr % width == 0 and tm % pr == 0
    n_pieces = tm // pr
    wpos = lax.broadcasted_iota(jnp.int32, (pr, FF_CHUNK), 0) & (width - 1)
    first_col = wpos == 0
    last_col = wpos == width - 1
    taps = (0, 1, 2) if vertical else (1,)

    def chunk(c):
        return slice(c * FF_CHUNK, (c + 1) * FF_CHUNK)

    def activation(c, a, g, r0):
        cw = cw_ref[:, chunk(c)]
        u = []
        for dw in range(3):
            s = None
            for dr in taps:
                term = g[dr * halo + r0:dr * halo + r0 + pr] * cw[3 * dr + dw:3 * dr + dw + 1]
                s = term if s is None else s + term
            u.append(s)
        conv = ((u[1] + cb_ref[:, chunk(c)]) + jnp.where(first_col, 0.0, pltpu.roll(u[0], 1, 0))
                + jnp.where(last_col, 0.0, pltpu.roll(u[2], pr - 1, 0)))
        th = jnp.tanh(conv * (GELU_K0 + GELU_K1 * (conv * conv)))
        return ((conv + conv * th) * a[r0:r0 + pr]).astype(BF16)

    def up_a(c):
        return _dot(hext_ref[halo:halo + tm, :], wa_ref[:, chunk(c)])

    def up_g(c):
        return _dot(hext_ref[...], wg_ref[:, chunk(c)])

    out_ref[...] = residual(x_ref, d_ref)
    hext_ref[halo:halo + tm, :] = hmod(out_ref[...])
    if vertical:
        zeros = jnp.zeros((halo, hext_ref.shape[1]), BF16)
        hext_ref[0:halo, :] = jnp.where(t == 0, zeros, hmod(residual(xu_ref, du_ref)))
        hext_ref[halo + tm:ext, :] = jnp.where(t == nt - 1, zeros, hmod(residual(xd_ref, dd_ref)))

    state = {"a": up_a(0), "g": up_g(0)}
    for c in range(N_FF_CHUNKS):
        a, g = state["a"], state["g"]
        matmuls = []
        if c + 1 < N_FF_CHUNKS:
            matmuls.append(lambda c=c: state.__setitem__("a", up_a(c + 1)))
            matmuls.append(lambda c=c: state.__setitem__("g", up_g(c + 1)))
        for j in range(n_pieces):
            act_ref[j * pr:(j + 1) * pr, chunk(c)] = activation(c, a, g, j * pr)
            if matmuls:
                matmuls.pop(0)()
        for m in matmuls:
            m()
    def down_piece(j):
        return _dot(act_ref[j * pr:(j + 1) * pr, :], wd_ref[...])

    p = down_piece(0)
    for j in range(n_pieces):
        p_next = down_piece(j + 1) if j + 1 < n_pieces else None
        rows = slice(j * pr, (j + 1) * pr)
        y = out_ref[rows, :] + g2_ref[...] * p
        if final_norm:
            ms = jnp.mean(y * y, axis=-1, keepdims=True)
            y = y * lax.rsqrt(ms + EPS) * fg_ref[...]
        out_ref[rows, :] = y
        p = p_next


def _conv_ffn(x, delta, sc, sh, g2, ng, wa, wg, cw, cb, wd, *, width, tm, vertical, final_g=None):
    bsz, seq, d = x.shape
    nt = seq // tm
    vec = pl.BlockSpec((None, 1, d), lambda b, t: (b, 0, 0))
    tile = pl.BlockSpec((None, tm, d), lambda b, t: (b, t, 0))
    halo_specs = []
    if vertical:
        rpt = tm // width
        nrows = seq // width
        halo_specs = [pl.BlockSpec((None, width, d), lambda b, t: (b, jnp.maximum(t * rpt - 1, 0), 0)),
                      pl.BlockSpec((None, width, d), lambda b, t: (b, jnp.minimum((t + 1) * rpt, nrows - 1), 0))]
    in_specs = [tile] + halo_specs
    args = [x] * (1 + len(halo_specs))
    if delta is not None:
        assert vertical
        in_specs += [tile] + halo_specs
        args += [delta] * 3
    in_specs += [vec, vec, vec, _full((1, d)), _resident(wa.shape), _resident(wg.shape),
                 _resident(cw.shape), _resident(cb.shape), _resident(wd.shape)]
    args += [sc, sh, g2, ng, wa, wg, cw, cb, wd]
    if final_g is not None:
        in_specs.append(_full((1, d)))
        args.append(final_g)
    ext = tm + (2 * width if vertical else 0)
    return pl.pallas_call(
        functools.partial(_ffn_kernel, width=width, tm=tm, vertical=vertical,
                          has_delta=delta is not None, final_norm=final_g is not None),
        grid=(bsz, nt),
        in_specs=in_specs,
        out_specs=tile,
        out_shape=jax.ShapeDtypeStruct(x.shape, F32),
        scratch_shapes=[pltpu.VMEM((ext, d), BF16), pltpu.VMEM((tm, wd.shape[0]), BF16)],
        compiler_params=_params(("parallel", "arbitrary")),
        name="conv_ffn_grid" if vertical else "conv_ffn_seq",
    )(*args)


def _ffn_weights(w_up, conv_w, conv_b, w_down):
    d = w_up.shape[0]

    assert w_up.shape == (d, 2 * D_FF)
    wa = w_up[:, :D_FF].astype(BF16)
    wg = w_up[:, D_FF:].astype(BF16)
    cw = conv_w.reshape(9, D_FF)
    cb = conv_b.reshape(1, D_FF)
    wd = (0.5 * w_down).astype(BF16)
    return wa, wg, cw, cb, wd


def kernel(x, c, ctx, c_ctx, ada_w, ada_b, norm_mix_g, norm_ffn_g, gla_w_in, gla_w_a2, gla_b_a2, gla_norm_g, gla_w_out, mlstm_w_in, mlstm_b_gate, mlstm_conv_w, mlstm_conv_b, mlstm_norm_g, mlstm_w_out, ffn_w_up, ffn_conv_w, ffn_conv_b, ffn_w_down, final_norm_g):
    bsz, seq, d = x.shape
    depth = ada_w.shape[0]
    assert bsz == 2 and d == D_MODEL and ctx.shape[1] == BLK and seq == BLK * GRID_W and depth == 2

    cvec = jnp.zeros((8, d), F32).at[:bsz].set(c).at[bsz].set(c_ctx)
    mods = _modulation(cvec, ada_w, ada_b)

    def mod(i, j):
        return mods[i, :3, j * d:(j + 1) * d].reshape(3, 1, d)

    def row(v):
        return v.reshape(1, -1)

    sh1, sc1, g1, sh2, sc2, g2 = (mod(0, j) for j in range(N_MOD))
    nsh1, nsc1, ng1, nsh2, nsc2, ng2 = (mod(1, j) for j in range(N_MOD))
    q, k, v, r, za = _gla_in(ctx, x, sc1, sh1, row(norm_mix_g[0]), gla_w_in[0].astype(BF16))
    o_f, o_b = _gla_scan(q, k, v, za, gla_w_a2[0], gla_b_a2[0])
    ctx, delta = _gla_out(o_f, o_b, r, ctx, g1, row(gla_norm_g[0]), gla_w_out[0].astype(BF16))
    ffn_w = _ffn_weights(ffn_w_up[0], ffn_conv_w[0], ffn_conv_b[0], ffn_w_down[0])
    x = _conv_ffn(x, delta, sc2[:bsz], sh2[:bsz], g2[:bsz], row(norm_ffn_g[0]), *ffn_w,
                  width=GRID_W, tm=FFN_ROWS * GRID_W, vertical=True)
    ctx = _conv_ffn(ctx, None, jnp.broadcast_to(sc2[2:], (bsz, 1, d)), jnp.broadcast_to(sh2[2:], (bsz, 1, d)),
                    jnp.broadcast_to(g2[2:], (bsz, 1, d)), row(norm_ffn_g[0]), *ffn_w,
                    width=BLK, tm=BLK, vertical=False)

    q, k, v, og, gt = _mlstm_in(ctx, x, nsc1, nsh1, row(norm_mix_g[1]), mlstm_w_in[0].astype(BF16),
                                mlstm_b_gate[0], mlstm_conv_w[0], mlstm_conv_b[0])
    h_f, h_b = _mlstm_scan(q, k, v, gt)
    delta = _mlstm_out(h_f, h_b, og, ng1[:bsz], row(mlstm_norm_g[0]), mlstm_w_out[0].astype(BF16))
    ffn_w = _ffn_weights(ffn_w_up[1], ffn_conv_w[1], ffn_conv_b[1], ffn_w_down[1])
    return _conv_ffn(x, delta, nsc2[:bsz], nsh2[:bsz], ng2[:bsz], row(norm_ffn_g[1]), *ffn_w,
                     width=GRID_W, tm=FFN_ROWS * GRID_W, vertical=True, final_g=row(final_norm_g))
```

```python
import functools

import jax
import jax.numpy as jnp
from jax import lax
from jax.experimental import pallas as pl
from jax.experimental.pallas import tpu as pltpu

D_MODEL = 1024
GRID_W = 64
N_HEADS = 4
D_K = 128
D_V = 256
QK_W = N_HEADS * D_K
V_W = N_HEADS * D_V
GLA_RANK = 16
GLA_TAU = 16.0
D_FF = 2816
N_MOD = 6
EPS = 1e-6

BLK = 256
SUB = 128
FF_CHUNK = 256
N_FF_CHUNKS = D_FF // FF_CHUNK
FFN_ROWS = 16
FFN_PIECE_ROWS = 256
VMEM_LIMIT_BYTES = 62 * 1024 * 1024

F32 = jnp.float32
BF16 = jnp.bfloat16


def _dot(a, b):
    return jnp.dot(a, b, preferred_element_type=F32)


def _dot_nt(a, b):
    return lax.dot_general(a, b, (((1,), (1,)), ((), ())), preferred_element_type=F32)


def _sigmoid(x):
    return 1.0 / (1.0 + jnp.exp(-x))


def _log_sigmoid(x):
    return jnp.minimum(x, 0.0) - jnp.log(1.0 + jnp.exp(-jnp.abs(x)))


def _rms_mod(x, ng, sc, sh):
    ms = jnp.mean(x * x, axis=-1, keepdims=True)
    return (x * lax.rsqrt(ms + EPS) * ng) * (1.0 + sc) + sh


def _split_bf16(x):
    hi = x.astype(BF16)
    lo = (x - hi.astype(F32)).astype(BF16)
    return hi, lo


def _params(sem):
    return pltpu.CompilerParams(dimension_semantics=sem, vmem_limit_bytes=VMEM_LIMIT_BYTES)


def _full(shape):
    n = len(shape)
    return pl.BlockSpec(shape, lambda *_: (0,) * n)


def _resident(shape):
    n = len(shape)
    return pl.BlockSpec(shape, lambda *_: (0,) * n, pipeline_mode=pl.Buffered(1))


def _sel_spec():
    return pl.BlockSpec((None, 1, D_MODEL), lambda b, t: (jnp.where(t == 0, 2, b), 0, 0))


def _mod_kernel(c_ref, w_ref, b_ref, o_ref):
    c = c_ref[...]
    s = c * _sigmoid(c)
    o_ref[...] = jnp.dot(s, w_ref[...], preferred_element_type=F32,
                         precision=lax.Precision.HIGHEST) + b_ref[...]


def _modulation(cvec, ada_w, ada_b):
    depth = ada_w.shape[0]
    n = N_MOD * D_MODEL
    tn = 1536
    return pl.pallas_call(
        _mod_kernel,
        grid=(depth, n // tn),
        in_specs=[pl.BlockSpec((8, D_MODEL), lambda i, j: (0, 0)),
                  pl.BlockSpec((None, D_MODEL, tn), lambda i, j: (i, 0, j)),
                  pl.BlockSpec((None, 1, tn), lambda i, j: (i, 0, j))],
        out_specs=pl.BlockSpec((None, 8, tn), lambda i, j: (i, 0, j)),
        out_shape=jax.ShapeDtypeStruct((depth, 8, n), F32),
        compiler_params=_params(("parallel", "parallel")),
        name="modulation",
    )(cvec, ada_w, ada_b.reshape(depth, 1, n))


def _load_tile(ctx_ref, x_ref):
    t = pl.program_id(1)
    return jnp.where(t == 0, ctx_ref[...], x_ref[...])


def _gla_in_kernel(ctx_ref, x_ref, sc_ref, sh_ref, ng_ref, w_ref,
                   q_ref, k_ref, v_ref, r_ref, za_ref):
    h = _rms_mod(_load_tile(ctx_ref, x_ref), ng_ref[...], sc_ref[...], sh_ref[...]).astype(BF16)
    q_ref[...] = (_dot(h, w_ref[:, 0:QK_W]) * D_K ** -0.5).astype(BF16)
    k_ref[...] = _dot(h, w_ref[:, QK_W:2 * QK_W]).astype(BF16)
    v_ref[...] = _dot(h, w_ref[:, 2 * QK_W:2 * QK_W + V_W]).astype(BF16)
    r_ref[...] = _dot(h, w_ref[:, 2 * QK_W + V_W:2 * QK_W + 2 * V_W]).astype(BF16)
    za_ref[...] = _dot(h, w_ref[:, 2 * QK_W + 2 * V_W:])


def _gla_in(ctx, x, sc, sh, ng, w_in):
    bsz, seq, d = x.shape
    nblk = 1 + seq // BLK
    t_tot = nblk * BLK
    n_in = w_in.shape[1]

    def tok(width):
        return pl.BlockSpec((None, BLK, width), lambda b, t: (b, t, 0))

    return pl.pallas_call(
        _gla_in_kernel,
        grid=(bsz, nblk),
        in_specs=[pl.BlockSpec((None, BLK, d), lambda b, t: (b, 0, 0)),
                  pl.BlockSpec((None, BLK, d), lambda b, t: (b, jnp.maximum(t - 1, 0), 0)),
                  _sel_spec(), _sel_spec(), _full((1, d)), _full((d, n_in))],
        out_specs=[tok(QK_W), tok(QK_W), tok(V_W), tok(V_W), tok(2 * GLA_RANK)],
        out_shape=[jax.ShapeDtypeStruct((bsz, t_tot, QK_W), BF16),
                   jax.ShapeDtypeStruct((bsz, t_tot, QK_W), BF16),
                   jax.ShapeDtypeStruct((bsz, t_tot, V_W), BF16),
                   jax.ShapeDtypeStruct((bsz, t_tot, V_W), BF16),
                   jax.ShapeDtypeStruct((bsz, t_tot, 2 * GLA_RANK), F32)],
        compiler_params=_params(("parallel", "arbitrary")),
        name="gla_in",
    )(ctx, x, sc, sh, ng, w_in)


def _mlstm_in_kernel(ctx_ref, x_ref, sc_ref, sh_ref, ng_ref, w_ref, wgt_ref, bgt_ref, cw_ref, cb_ref,
                     q_ref, k_ref, v_ref, og_ref, gt_ref, pre_ref, last_ref, *, nblk):
    t = pl.program_id(1)

    @pl.when(t == 0)
    def _():
        last_ref[...] = jnp.zeros_like(last_ref)
        pre_ref[...] = jnp.zeros_like(pre_ref)

    h = _rms_mod(_load_tile(ctx_ref, x_ref), ng_ref[...], sc_ref[...], sh_ref[...]).astype(BF16)

    blk = t - 1
    x = pre_ref[(t + 1) % 2]
    pre = _dot(h, w_ref[:, 0:2 * QK_W])
    pre_ref[t % 2] = pre
    has_prev = blk > 1
    has_next = jnp.logical_and(blk > 0, blk != nblk - 1)
    prev = jnp.where(has_prev, last_ref[0:1, :], 0.0)
    nxt = jnp.where(has_next, pre[0:1, :], 0.0)
    row = lax.broadcasted_iota(jnp.int32, (BLK, 1), 0)
    x_m1 = jnp.where(row == 0, prev, pltpu.roll(x, 1, 0))
    x_p1 = jnp.where(row == BLK - 1, nxt, pltpu.roll(x, BLK - 1, 0))
    last_ref[0:1, :] = x[BLK - 1:BLK, :]

    def conv_piece(r0, r1):
        qk = (x_m1[r0:r1] * cw_ref[0:1, :] + x[r0:r1] * cw_ref[1:2, :] + x_p1[r0:r1] * cw_ref[2:3, :]
              + cb_ref[...])
        qk = qk * _sigmoid(qk)
        q_ref[r0:r1, :] = qk[:, :QK_W].astype(BF16)
        k_ref[r0:r1, :] = (qk[:, QK_W:] * D_K ** -0.5).astype(BF16)

    def gates():
        gt = _dot_nt(wgt_ref[...], h) + bgt_ref[...]
        is_forget = lax.broadcasted_iota(jnp.int32, gt.shape, 0) >= 2 * N_HEADS
        gt_ref[...] = jnp.where(is_forget, _log_sigmoid(gt), gt)

    n_pieces = 4
    step = BLK // n_pieces
    conv_piece(0, step)
    v_ref[...] = _dot(h, w_ref[:, 2 * QK_W:2 * QK_W + V_W]).astype(BF16)
    conv_piece(step, 2 * step)
    og_ref[...] = _dot(h, w_ref[:, 2 * QK_W + V_W:2 * QK_W + 2 * V_W]).astype(BF16)
    conv_piece(2 * step, 3 * step)
    gates()
    conv_piece(3 * step, BLK)


def _mlstm_in(ctx, x, sc, sh, ng, w_in, b_gate, conv_w, conv_b):
    bsz, seq, d = x.shape
    rows = seq // GRID_W
    assert rows == BLK
    nblk = 1 + GRID_W
    t_tot = nblk * BLK
    n_in = w_in.shape[1]
    n_g = 4 * N_HEADS
    xcol = x.reshape(bsz, rows, GRID_W * d)
    w_gt = w_in[:, 2 * QK_W + 2 * V_W:].T

    def late(width):
        return pl.BlockSpec((None, BLK, width), lambda b, t: (b, jnp.maximum(t - 1, 0), 0))

    def tok(width):
        return pl.BlockSpec((None, BLK, width), lambda b, t: (b, jnp.minimum(t, nblk - 1), 0))

    return pl.pallas_call(
        functools.partial(_mlstm_in_kernel, nblk=nblk),
        grid=(bsz, nblk + 1),
        in_specs=[pl.BlockSpec((None, BLK, d), lambda b, t: (b, 0, 0)),
                  pl.BlockSpec((None, rows, d), lambda b, t: (b, 0, jnp.clip(t - 1, 0, GRID_W - 1))),
                  _sel_spec(), _sel_spec(), _full((1, d)), _full((d, n_in)), _full((n_g, d)),
                  _full((n_g, 1)), _full((3, 2 * QK_W)), _full((1, 2 * QK_W))],
        out_specs=[late(QK_W), late(QK_W), tok(V_W), tok(V_W),
                   pl.BlockSpec((None, n_g, BLK), lambda b, t: (b, 0, jnp.minimum(t, nblk - 1)))],
        out_shape=[jax.ShapeDtypeStruct((bsz, t_tot, QK_W), BF16),
                   jax.ShapeDtypeStruct((bsz, t_tot, QK_W), BF16),
                   jax.ShapeDtypeStruct((bsz, t_tot, V_W), BF16),
                   jax.ShapeDtypeStruct((bsz, t_tot, V_W), BF16),
                   jax.ShapeDtypeStruct((bsz, n_g, t_tot), F32)],
        scratch_shapes=[pltpu.VMEM((2, BLK, 2 * QK_W), F32), pltpu.VMEM((8, 2 * QK_W), F32)],
        compiler_params=_params(("parallel", "arbitrary")),
        name="mlstm_in",
    )(ctx, xcol, sc, sh, ng, w_in, w_gt, b_gate.reshape(n_g, 1), conv_w, conv_b.reshape(1, 2 * QK_W))


def _block_tri(n, sub, lower):
    ri = lax.broadcasted_iota(jnp.int32, (n, n), 0)
    ci = lax.broadcasted_iota(jnp.int32, (n, n), 1)
    shift = sub.bit_length() - 1
    assert sub == 1 << shift
    same = (ri >> shift) == (ci >> shift)
    tri = (ci <= ri) if lower else (ci >= ri)
    return jnp.where(same & tri, 1.0, 0.0).astype(BF16)


def _tri_mask(n, lower):
    ri = lax.broadcasted_iota(jnp.int32, (n, n), 0)
    ci = lax.broadcasted_iota(jnp.int32, (n, n), 1)
    return (ci <= ri) if lower else (ci >= ri)


def _staggered_round_robin(groups, lag):
    live, rnd = [], 0
    while live or rnd <= lag * (len(groups) - 1):
        if rnd % lag == 0 and rnd // lag < len(groups):
            live = live + list(groups[rnd // lag])
        alive = []
        for gen in live:
            try:
                next(gen)
                alive.append(gen)
            except StopIteration:
                pass
        live, rnd = alive, rnd + 1


def _gla_scan_kernel(qf, kf, vf, zf, qb, kb, vb, zb, w2_ref, b2_ref, of_ref, ob_ref, *s_refs):
    @pl.when(pl.program_id(1) == 0)
    def _():
        for s_ref in s_refs:
            s_ref[...] = jnp.zeros_like(s_ref)

    dirs = ((qf, kf, vf, zf, of_ref), (qb, kb, vb, zb, ob_ref))
    tris = (_block_tri(BLK, SUB, lower=True), _block_tri(BLK, SUB, lower=False))
    masks = (_tri_mask(SUB, lower=True), _tri_mask(SUB, lower=False))

    def unit(d, h):
        q_ref, k_ref, v_ref, z_ref, o_ref = dirs[d]
        s_ref = s_refs[d * N_HEADS + h]
        fwd = d == 0
        ks = slice(h * D_K, (h + 1) * D_K)
        vs = slice(h * D_V, (h + 1) * D_V)
        za = z_ref[:, d * GLA_RANK:(d + 1) * GLA_RANK].astype(BF16)
        y = _dot(za, w2_ref[d, :, ks])
        yield
        la = _log_sigmoid(y + b2_ref[d, :, ks]) * (1.0 / GLA_TAU)
        hi, lo = _split_bf16(la)
        bcum = _dot(tris[d], jnp.concatenate([hi, lo], axis=1))
        yield
        bcum = bcum[:, :D_K] + bcum[:, D_K:]
        ops = {}
        for sub in (0, 1):
            rs = slice(sub * SUB, (sub + 1) * SUB)
            bs = bcum[rs]
            tot = bs[SUB - 1:SUB] if fwd else bs[0:1]
            mid = 0.5 * tot
            e_mid = jnp.exp(mid)
            q_mid = q_ref[rs, ks].astype(F32) * jnp.exp(bs - mid)
            k_mid = k_ref[rs, ks].astype(F32) * jnp.exp(mid - bs)
            q_in = (q_mid * e_mid).astype(BF16)
            k_st_t = (k_mid * e_mid).T.astype(BF16)
            ops[sub] = (q_mid.astype(BF16), q_in, k_mid.astype(BF16), k_st_t, e_mid * e_mid)
            yield
        for sub in ((0, 1) if fwd else (1, 0)):
            rs = slice(sub * SUB, (sub + 1) * SUB)
            q_mid, q_in, k_mid, k_st_t, decay = ops[sub]
            scores = _dot_nt(q_mid, k_mid)
            yield
            p = jnp.where(masks[d], scores, 0.0).astype(BF16)
            vh = v_ref[rs, vs]
            s_old = s_ref[...]
            o = _dot(jnp.concatenate([p, q_in], axis=1), jnp.concatenate([vh, s_old.astype(BF16)], axis=0))
            upd = _dot(k_st_t, vh)
            yield
            o_ref[rs, vs] = o.astype(o_ref.dtype)
            dcol = jnp.broadcast_to(decay, (D_K, D_K)).T
            s_ref[...] = s_old * jnp.concatenate([dcol, dcol], axis=1) + upd
            yield

    _staggered_round_robin([[unit(d, h) for h in hs for d in (0, 1)] for hs in ((0, 1), (2, 3))], lag=3)


def _scan_maps(nblk):
    def fwd(b, s):
        return (b, s, 0)

    def bwd(b, s):
        return (b, jnp.where(s == 0, 0, nblk - s), 0)

    return fwd, bwd


def _gla_scan(q, k, v, za, w_a2, b_a2):
    bsz, t_tot, _ = q.shape
    nblk = t_tot // BLK
    fwd, bwd = _scan_maps(nblk)

    def specs(m):
        return [pl.BlockSpec((None, BLK, QK_W), m), pl.BlockSpec((None, BLK, QK_W), m),
                pl.BlockSpec((None, BLK, V_W), m), pl.BlockSpec((None, BLK, 2 * GLA_RANK), m)]

    return pl.pallas_call(
        _gla_scan_kernel,
        grid=(bsz, nblk),
        in_specs=specs(fwd) + specs(bwd) + [_full((2, GLA_RANK, QK_W)), _full((2, 1, QK_W))],
        out_specs=[pl.BlockSpec((None, BLK, V_W), fwd), pl.BlockSpec((None, BLK, V_W), bwd)],
        out_shape=[jax.ShapeDtypeStruct((bsz, t_tot, V_W), BF16)] * 2,
        scratch_shapes=[pltpu.VMEM((D_K, D_V), F32)] * (2 * N_HEADS),
        compiler_params=_params(("parallel", "arbitrary")),
        name="gla_scan",
    )(q, k, v, za, q, k, v, za, w_a2.astype(BF16), b_a2.reshape(2, 1, QK_W))


def _mlstm_scan_kernel(qf, kf, vf, gtf, qb, kb, vb, gtb, of_ref, ob_ref, *state_refs):
    n_units = 2 * N_HEADS
    c_refs, nb_refs, m_refs = (state_refs[i * n_units:(i + 1) * n_units] for i in range(3))

    @pl.when(pl.program_id(1) == 0)
    def _():
        for ref in state_refs:
            ref[...] = jnp.zeros_like(ref)

    ones = jnp.ones((SUB, D_K), BF16)
    dirs = ((qf, kf, vf, gtf, of_ref), (qb, kb, vb, gtb, ob_ref))
    gates = []
    for d, (_, _, _, gt_ref, _) in enumerate(dirs):
        gt = gt_ref[...]
        gt_hi, gt_lo = _split_bf16(gt)
        tri_r = _block_tri(BLK, SUB, lower=d != 0)
        gates.append((gt, _dot(gt_hi, tri_r) + _dot(gt_lo, tri_r)))
    masks = (_tri_mask(SUB, lower=True), _tri_mask(SUB, lower=False))
    def unit(d, h):
        q_ref, k_ref, v_ref, _, o_ref = dirs[d]
        fwd = d == 0
        gt, f_row = gates[d]
        ks = slice(h * D_K, (h + 1) * D_K)
        vs = slice(h * D_V, (h + 1) * D_V)
        gi = d * N_HEADS + h
        gf_ = 2 * N_HEADS + gi
        c_ref, nb_ref, m_ref = c_refs[gi], nb_refs[gi], m_refs[gi]
        pre = {}
        for sub in (0, 1):
            rs = slice(sub * SUB, (sub + 1) * SUB)
            fc_r = f_row[gf_:gf_ + 1, rs]
            ic_r = gt[gi:gi + 1, rs]
            fc_b = jnp.broadcast_to(fc_r, (SUB, SUB)).T
            d_mat = jnp.where(masks[d], fc_b + (ic_r - fc_r), -jnp.inf)
            d_max = jnp.max(d_mat, axis=-1, keepdims=True)
            qk = _dot_nt(q_ref[rs, ks], k_ref[rs, ks])
            k_t = k_ref[rs, ks].astype(F32).T
            pre[sub] = (fc_r, ic_r, fc_b, d_mat, d_max, qk, k_t)
            yield
        for sub in ((0, 1) if fwd else (1, 0)):
            rs = slice(sub * SUB, (sub + 1) * SUB)
            fc_r, ic_r, fc_b, d_mat, d_max, qk, k_t = pre[sub]
            vh = v_ref[rs, vs]
            c_old = c_ref[...]
            n_old = nb_ref[...]
            m_old = m_ref[...]
            inter = fc_b + m_old
            m_i = jnp.maximum(inter, d_max)
            w_inter = jnp.exp(inter - m_i)
            s_mat = qk * jnp.exp(d_mat - m_i)
            lhs = jnp.concatenate([s_mat.astype(BF16),
                                   (w_inter * q_ref[rs, ks].astype(F32)).astype(BF16)], axis=1)
            num = _dot(lhs, jnp.concatenate([vh, c_old.astype(BF16)], axis=0))
            den = _dot(lhs, jnp.concatenate([ones, n_old.astype(BF16)], axis=0))
            f_last = fc_r[:, SUB - 1:SUB] if fwd else fc_r[:, 0:1]
            g_r = f_last - fc_r + ic_r
            m_new = jnp.maximum(f_last + m_old, jnp.max(g_r, axis=-1, keepdims=True))
            dec = jnp.exp(f_last + m_old - m_new)[:, 0:1]
            kw_t = (k_t * jnp.exp(g_r - m_new)).astype(BF16)
            c_upd = _dot(kw_t, vh)
            n_upd = _dot(kw_t, ones)
            yield
            r = 1.0 / jnp.maximum(jnp.abs(den), jnp.exp(-m_i))
            o_ref[rs, vs] = (num * jnp.concatenate([r, r], axis=1)).astype(o_ref.dtype)
            c_ref[...] = dec * c_old + c_upd
            nb_ref[...] = dec * n_old + n_upd
            m_ref[...] = m_new
            yield

    _staggered_round_robin([[unit(d, h) for h in hs for d in (0, 1)] for hs in ((0, 1), (2, 3))], lag=2)


def _mlstm_scan(q, k, v, gt):
    bsz, t_tot, _ = v.shape
    nblk = t_tot // BLK
    fwd, bwd = _scan_maps(nblk)
    n_g = 4 * N_HEADS

    def specs(m):
        return [pl.BlockSpec((None, BLK, QK_W), m), pl.BlockSpec((None, BLK, QK_W), m),
                pl.BlockSpec((None, BLK, V_W), m),
                pl.BlockSpec((None, n_g, BLK), lambda b, s: (b, 0, m(b, s)[1]))]

    return pl.pallas_call(
        _mlstm_scan_kernel,
        grid=(bsz, nblk),
        in_specs=specs(fwd) + specs(bwd),
        out_specs=[pl.BlockSpec((None, BLK, V_W), fwd), pl.BlockSpec((None, BLK, V_W), bwd)],
        out_shape=[jax.ShapeDtypeStruct((bsz, t_tot, V_W), BF16)] * 2,
        scratch_shapes=([pltpu.VMEM((D_K, D_V), F32)] * (2 * N_HEADS)
                        + [pltpu.VMEM((D_K, D_K), F32)] * (2 * N_HEADS)
                        + [pltpu.VMEM((1, D_K), F32)] * (2 * N_HEADS)),
        compiler_params=_params(("parallel", "arbitrary")),
        name="mlstm_scan",
    )(q, k, v, gt, q, k, v, gt)


OUT_PIECES = 2


def _head_norm_gate(of_ref, ob_ref, gate, ng_ref, rows):
    o = of_ref[rows, :].astype(F32) + ob_ref[rows, :].astype(F32)
    parts = []
    for h in range(N_HEADS):
        oh = o[:, h * D_V:(h + 1) * D_V]
        ms = jnp.mean(oh * oh, axis=-1, keepdims=True)
        parts.append(oh * lax.rsqrt(ms + EPS))
    return (jnp.concatenate(parts, axis=1) * ng_ref[...] * gate).astype(BF16)


def _mixer_out_pieces(gated, w_ref, finish):
    pr = BLK // OUT_PIECES
    rows = [slice(j * pr, (j + 1) * pr) for j in range(OUT_PIECES)]
    nxt = gated(rows[0])
    for j in range(OUT_PIECES):
        y = _dot(nxt, w_ref[...])
        if j + 1 < OUT_PIECES:
            nxt = gated(rows[j + 1])
        finish(rows[j], y)


def _gla_out_kernel(of_ref, ob_ref, r_ref, ctx_ref, g1_ref, ng_ref, w_ref, ctx_out, delta_out):
    t = pl.program_id(1)

    def gated(rows):
        r = r_ref[rows, :].astype(F32)
        return _head_norm_gate(of_ref, ob_ref, r * _sigmoid(r), ng_ref, rows)

    ys = []

    def finish(rows, y):
        delta_out[rows, :] = (g1_ref[...] * y).astype(delta_out.dtype)
        ys.append((rows, y))

    _mixer_out_pieces(gated, w_ref, finish)

    @pl.when(t == 0)
    def _():
        for rows, y in ys:
            ctx_out[rows, :] = ctx_ref[rows, :] + g1_ref[...] * y


def _gla_out(o_f, o_b, r, ctx, g1, ng, w_out):
    bsz, t_tot, _ = o_f.shape
    d = ctx.shape[2]
    nblk = t_tot // BLK
    tok = pl.BlockSpec((None, BLK, V_W), lambda b, t: (b, t, 0))
    ctx_spec = pl.BlockSpec((None, BLK, d), lambda b, t: (b, 0, 0))
    lat_spec = pl.BlockSpec((None, BLK, d), lambda b, t: (b, jnp.maximum(t - 1, 0), 0))
    return pl.pallas_call(
        _gla_out_kernel,
        grid=(bsz, nblk),
        in_specs=[tok, tok, tok, ctx_spec, _sel_spec(), _full((1, V_W)), _full((V_W, d))],
        out_specs=[ctx_spec, lat_spec],
        out_shape=[jax.ShapeDtypeStruct(ctx.shape, F32),
                   jax.ShapeDtypeStruct((bsz, t_tot - BLK, d), BF16)],
        compiler_params=_params(("parallel", "arbitrary")),
        name="gla_out",
    )(o_f, o_b, r, ctx, g1, ng, w_out)


MLSTM_OUT_COLS = 2


def _mlstm_out_kernel(of_ref, ob_ref, og_ref, g1_ref, ng_ref, w_ref, delta_out):
    d = w_ref.shape[1]
    of_ref, ob_ref, og_ref = of_ref.at[0], ob_ref.at[0], og_ref.at[0]

    def gated(j):
        rows = slice(j * BLK, (j + 1) * BLK)
        return _head_norm_gate(of_ref, ob_ref, _sigmoid(og_ref[rows, :].astype(F32)), ng_ref, rows)

    nxt = gated(0)
    for j in range(MLSTM_OUT_COLS):
        y = _dot(nxt, w_ref[...])
        if j + 1 < MLSTM_OUT_COLS:
            nxt = gated(j + 1)
        delta_out[:, j * d:(j + 1) * d] = (g1_ref[...] * y).astype(delta_out.dtype)


def _mlstm_out(h_f, h_b, og, g1, ng, w_out):
    bsz, t_tot, _ = h_f.shape
    d = w_out.shape[1]
    seq = t_tot - BLK
    rows = seq // GRID_W
    cols = MLSTM_OUT_COLS
    tok = pl.BlockSpec((pl.Element(1), pl.Element(cols * BLK), pl.Element(V_W)),
                       lambda b, c: (b, pl.multiple_of(BLK + cols * BLK * c, BLK), 0))
    out = pl.pallas_call(
        _mlstm_out_kernel,
        grid=(bsz, GRID_W // cols),
        in_specs=[tok, tok, tok,
                  pl.BlockSpec((None, 1, d), lambda b, c: (b, 0, 0)),
                  _full((1, V_W)), _full((V_W, d))],
        out_specs=pl.BlockSpec((None, rows, cols * d), lambda b, c: (b, 0, c)),
        out_shape=jax.ShapeDtypeStruct((bsz, rows, GRID_W * d), BF16),
        compiler_params=_params(("parallel", "arbitrary")),
        name="mlstm_out",
    )(h_f, h_b, og, g1, ng, w_out)
    return out.reshape(bsz, seq, d)


GELU_K0 = 0.7978845608028654
GELU_K1 = GELU_K0 * 0.044715


def _ffn_kernel(*refs, width, tm, vertical, has_delta, final_norm):
    refs = list(refs)
    x_ref = refs.pop(0)
    xu_ref, xd_ref = (refs.pop(0), refs.pop(0)) if vertical else (None, None)
    d_ref, du_ref, dd_ref = (refs.pop(0), refs.pop(0), refs.pop(0)) if has_delta else (None,) * 3
    sc_ref, sh_ref, g2_ref, ng_ref, wa_ref, wg_ref, cw_ref, cb_ref, wd_ref = refs[:9]
    refs = refs[9:]
    fg_ref = refs.pop(0) if final_norm else None
    out_ref, hext_ref, act_ref = refs

    def residual(x_r, delta_r):
        return x_r[...] + delta_r[...].astype(F32) if has_delta else x_r[...]

    t = pl.program_id(1)
    nt = pl.num_programs(1)
    halo = width if vertical else 0
    ext = tm + 2 * halo

    def hmod(xt):
        return _rms_mod(xt, ng_ref[...], sc_ref[...], sh_ref[...]).astype(BF16)

    assert width & (width - 1) == 0
    pr = max(width, min(tm, FFN_PIECE_ROWS))
    assert pr % width == 0 and tm % pr == 0
    n_pieces = tm // pr
    wpos = lax.broadcasted_iota(jnp.int32, (pr, FF_CHUNK), 0) & (width - 1)
    first_col = wpos == 0
    last_col = wpos == width - 1
    taps = (0, 1, 2) if vertical else (1,)

    def chunk(c):
        return slice(c * FF_CHUNK, (c + 1) * FF_CHUNK)

    def activation(c, a, g, r0):
        cw = cw_ref[:, chunk(c)]
        u = []
        for dw in range(3):
            s = None
            for dr in taps:
                term = g[dr * halo + r0:dr * halo + r0 + pr] * cw[3 * dr + dw:3 * dr + dw + 1]
                s = term if s is None else s + term
            u.append(s)
        conv = ((u[1] + cb_ref[:, chunk(c)]) + jnp.where(first_col, 0.0, pltpu.roll(u[0], 1, 0))
                + jnp.where(last_col, 0.0, pltpu.roll(u[2], pr - 1, 0)))
        th = jnp.tanh(conv * (GELU_K0 + GELU_K1 * (conv * conv)))
        return ((conv + conv * th) * a[r0:r0 + pr]).astype(BF16)

    def up_a(c):
        return _dot(hext_ref[halo:halo + tm, :], wa_ref[:, chunk(c)])

    def up_g(c):
        return _dot(hext_ref[...], wg_ref[:, chunk(c)])

    out_ref[...] = residual(x_ref, d_ref)
    hext_ref[halo:halo + tm, :] = hmod(out_ref[...])
    if vertical:
        zeros = jnp.zeros((halo, hext_ref.shape[1]), BF16)
        hext_ref[0:halo, :] = jnp.where(t == 0, zeros, hmod(residual(xu_ref, du_ref)))
        hext_ref[halo + tm:ext, :] = jnp.where(t == nt - 1, zeros, hmod(residual(xd_ref, dd_ref)))

    state = {"a": up_a(0), "g": up_g(0)}
    for c in range(N_FF_CHUNKS):
        a, g = state["a"], state["g"]
        matmuls = []
        if c + 1 < N_FF_CHUNKS:
            matmuls.append(lambda c=c: state.__setitem__("a", up_a(c + 1)))
            matmuls.append(lambda c=c: state.__setitem__("g", up_g(c + 1)))
        for j in range(n_pieces):
            if matmuls:
                matmuls.pop(0)()
            act_ref[j * pr:(j + 1) * pr, chunk(c)] = activation(c, a, g, j * pr)
        for m in matmuls:
            m()
    y = out_ref[...] + g2_ref[...] * _dot(act_ref[...], wd_ref[...])
    if final_norm:
        ms = jnp.mean(y * y, axis=-1, keepdims=True)
        y = y * lax.rsqrt(ms + EPS) * fg_ref[...]
    out_ref[...] = y


def _conv_ffn(x, delta, sc, sh, g2, ng, wa, wg, cw, cb, wd, *, width, tm, vertical, final_g=None):
    bsz, seq, d = x.shape
    nt = seq // tm
    vec = pl.BlockSpec((None, 1, d), lambda b, t: (b, 0, 0))
    tile = pl.BlockSpec((None, tm, d), lambda b, t: (b, t, 0))
    halo_specs = []
    if vertical:
        rpt = tm // width
        nrows = seq // width
        halo_specs = [pl.BlockSpec((None, width, d), lambda b, t: (b, jnp.maximum(t * rpt - 1, 0), 0)),
                      pl.BlockSpec((None, width, d), lambda b, t: (b, jnp.minimum((t + 1) * rpt, nrows - 1), 0))]
    in_specs = [tile] + halo_specs
    args = [x] * (1 + len(halo_specs))
    if delta is not None:
        assert vertical
        in_specs += [tile] + halo_specs
        args += [delta] * 3
    in_specs += [vec, vec, vec, _full((1, d)), _resident(wa.shape), _resident(wg.shape),
                 _resident(cw.shape), _resident(cb.shape), _resident(wd.shape)]
    args += [sc, sh, g2, ng, wa, wg, cw, cb, wd]
    if final_g is not None:
        in_specs.append(_full((1, d)))
        args.append(final_g)
    ext = tm + (2 * width if vertical else 0)
    return pl.pallas_call(
        functools.partial(_ffn_kernel, width=width, tm=tm, vertical=vertical,
                          has_delta=delta is not None, final_norm=final_g is not None),
        grid=(bsz, nt),
        in_specs=in_specs,
        out_specs=tile,
        out_shape=jax.ShapeDtypeStruct(x.shape, F32),
        scratch_shapes=[pltpu.VMEM((ext, d), BF16), pltpu.VMEM((tm, wd.shape[0]), BF16)],
        compiler_params=_params(("parallel", "arbitrary")),
        name="conv_ffn_grid" if vertical else "conv_ffn_seq",
    )(*args)


def _ffn_weights(w_up, conv_w, conv_b, w_down):
    d = w_up.shape[0]

    assert w_up.shape == (d, 2 * D_FF)
    wa = w_up[:, :D_FF].astype(BF16)
    wg = w_up[:, D_FF:].astype(BF16)
    cw = conv_w.reshape(9, D_FF)
    cb = conv_b.reshape(1, D_FF)
    wd = (0.5 * w_down).astype(BF16)
    return wa, wg, cw, cb, wd


def kernel(x, c, ctx, c_ctx, ada_w, ada_b, norm_mix_g, norm_ffn_g, gla_w_in, gla_w_a2, gla_b_a2, gla_norm_g, gla_w_out, mlstm_w_in, mlstm_b_gate, mlstm_conv_w, mlstm_conv_b, mlstm_norm_g, mlstm_w_out, ffn_w_up, ffn_conv_w, ffn_conv_b, ffn_w_down, final_norm_g):
    bsz, seq, d = x.shape
    depth = ada_w.shape[0]
    assert bsz == 2 and d == D_MODEL and ctx.shape[1] == BLK and seq == BLK * GRID_W and depth == 2

    cvec = jnp.zeros((8, d), F32).at[:bsz].set(c).at[bsz].set(c_ctx)
    mods = _modulation(cvec, ada_w, ada_b)

    def mod(i, j):
        return mods[i, :3, j * d:(j + 1) * d].reshape(3, 1, d)

    def row(v):
        return v.reshape(1, -1)

    sh1, sc1, g1, sh2, sc2, g2 = (mod(0, j) for j in range(N_MOD))
    nsh1, nsc1, ng1, nsh2, nsc2, ng2 = (mod(1, j) for j in range(N_MOD))
    q, k, v, r, za = _gla_in(ctx, x, sc1, sh1, row(norm_mix_g[0]), gla_w_in[0].astype(BF16))
    o_f, o_b = _gla_scan(q, k, v, za, gla_w_a2[0], gla_b_a2[0])
    ctx, delta = _gla_out(o_f, o_b, r, ctx, g1, row(gla_norm_g[0]), gla_w_out[0].astype(BF16))
    ffn_w = _ffn_weights(ffn_w_up[0], ffn_conv_w[0], ffn_conv_b[0], ffn_w_down[0])
    x = _conv_ffn(x, delta, sc2[:bsz], sh2[:bsz], g2[:bsz], row(norm_ffn_g[0]), *ffn_w,
                  width=GRID_W, tm=FFN_ROWS * GRID_W, vertical=True)
    ctx = _conv_ffn(ctx, None, jnp.broadcast_to(sc2[2:], (bsz, 1, d)), jnp.broadcast_to(sh2[2:], (bsz, 1, d)),
                    jnp.broadcast_to(g2[2:], (bsz, 1, d)), row(norm_ffn_g[0]), *ffn_w,
                    width=BLK, tm=BLK, vertical=False)

    q, k, v, og, gt = _mlstm_in(ctx, x, nsc1, nsh1, row(norm_mix_g[1]), mlstm_w_in[0].astype(BF16),
                                mlstm_b_gate[0], mlstm_conv_w[0], mlstm_conv_b[0])
    h_f, h_b = _mlstm_scan(q, k, v, gt)
    delta = _mlstm_out(h_f, h_b, og, ng1[:bsz], row(mlstm_norm_g[0]), mlstm_w_out[0].astype(BF16))
    ffn_w = _ffn_weights(ffn_w_up[1], ffn_conv_w[1], ffn_conv_b[1], ffn_w_down[1])
    return _conv_ffn(x, delta, nsc2[:bsz], nsh2[:bsz], ng2[:bsz], row(norm_ffn_g[1]), *ffn_w,
                     width=GRID_W, tm=FFN_ROWS * GRID_W, vertical=True, final_g=row(final_norm_g))
```

```python
import functools

import jax
import jax.numpy as jnp
from jax import lax
from jax.experimental import pallas as pl
from jax.experimental.pallas import tpu as pltpu

D_MODEL = 1024
GRID_W = 64
N_HEADS = 4
D_K = 128
D_V = 256
QK_W = N_HEADS * D_K
V_W = N_HEADS * D_V
GLA_RANK = 16
GLA_TAU = 16.0
D_FF = 2816
N_MOD = 6
EPS = 1e-6

BLK = 256
SUB = 128
FF_CHUNK = 256
N_FF_CHUNKS = D_FF // FF_CHUNK
FFN_ROWS = 16
FFN_PIECE_ROWS = 256
VMEM_LIMIT_BYTES = 62 * 1024 * 1024

F32 = jnp.float32
BF16 = jnp.bfloat16


def _dot(a, b):
    return jnp.dot(a, b, preferred_element_type=F32)


def _dot_nt(a, b):
    return lax.dot_general(a, b, (((1,), (1,)), ((), ())), preferred_element_type=F32)


def _sigmoid(x):
    return 1.0 / (1.0 + jnp.exp(-x))


def _log_sigmoid(x):
    return jnp.minimum(x, 0.0) - jnp.log(1.0 + jnp.exp(-jnp.abs(x)))


def _rms_mod(x, ng, sc, sh):
    ms = jnp.mean(x * x, axis=-1, keepdims=True)
    return (x * lax.rsqrt(ms + EPS) * ng) * (1.0 + sc) + sh


def _split_bf16(x):
    hi = x.astype(BF16)
    lo = (x - hi.astype(F32)).astype(BF16)
    return hi, lo


def _params(sem):
    return pltpu.CompilerParams(dimension_semantics=sem, vmem_limit_bytes=VMEM_LIMIT_BYTES)


def _full(shape):
    n = len(shape)
    return pl.BlockSpec(shape, lambda *_: (0,) * n)


def _resident(shape):
    n = len(shape)
    return pl.BlockSpec(shape, lambda *_: (0,) * n, pipeline_mode=pl.Buffered(1))


def _sel_spec():
    return pl.BlockSpec((None, 1, D_MODEL), lambda b, t: (jnp.where(t == 0, 2, b), 0, 0))


def _mod_kernel(c_ref, w_ref, b_ref, o_ref):
    c = c_ref[...]
    s = c * _sigmoid(c)
    o_ref[...] = jnp.dot(s, w_ref[...], preferred_element_type=F32,
                         precision=lax.Precision.HIGHEST) + b_ref[...]


def _modulation(cvec, ada_w, ada_b):
    depth = ada_w.shape[0]
    n = N_MOD * D_MODEL
    tn = 1536
    return pl.pallas_call(
        _mod_kernel,
        grid=(depth, n // tn),
        in_specs=[pl.BlockSpec((8, D_MODEL), lambda i, j: (0, 0)),
                  pl.BlockSpec((None, D_MODEL, tn), lambda i, j: (i, 0, j)),
                  pl.BlockSpec((None, 1, tn), lambda i, j: (i, 0, j))],
        out_specs=pl.BlockSpec((None, 8, tn), lambda i, j: (i, 0, j)),
        out_shape=jax.ShapeDtypeStruct((depth, 8, n), F32),
        compiler_params=_params(("parallel", "parallel")),
        name="modulation",
    )(cvec, ada_w, ada_b.reshape(depth, 1, n))


def _load_tile(ctx_ref, x_ref):
    t = pl.program_id(1)
    return jnp.where(t == 0, ctx_ref[...], x_ref[...])


def _gla_in_kernel(ctx_ref, x_ref, sc_ref, sh_ref, ng_ref, w_ref,
                   q_ref, k_ref, v_ref, r_ref, za_ref):
    h = _rms_mod(_load_tile(ctx_ref, x_ref), ng_ref[...], sc_ref[...], sh_ref[...]).astype(BF16)
    q_ref[...] = (_dot(h, w_ref[:, 0:QK_W]) * D_K ** -0.5).astype(BF16)
    k_ref[...] = _dot(h, w_ref[:, QK_W:2 * QK_W]).astype(BF16)
    v_ref[...] = _dot(h, w_ref[:, 2 * QK_W:2 * QK_W + V_W]).astype(BF16)
    r_ref[...] = _dot(h, w_ref[:, 2 * QK_W + V_W:2 * QK_W + 2 * V_W]).astype(BF16)
    za_ref[...] = _dot(h, w_ref[:, 2 * QK_W + 2 * V_W:])


def _gla_in(ctx, x, sc, sh, ng, w_in):
    bsz, seq, d = x.shape
    nblk = 1 + seq // BLK
    t_tot = nblk * BLK
    n_in = w_in.shape[1]

    def tok(width):
        return pl.BlockSpec((None, BLK, width), lambda b, t: (b, t, 0))

    return pl.pallas_call(
        _gla_in_kernel,
        grid=(bsz, nblk),
        in_specs=[pl.BlockSpec((None, BLK, d), lambda b, t: (b, 0, 0)),
                  pl.BlockSpec((None, BLK, d), lambda b, t: (b, jnp.maximum(t - 1, 0), 0)),
                  _sel_spec(), _sel_spec(), _full((1, d)), _full((d, n_in))],
        out_specs=[tok(QK_W), tok(QK_W), tok(V_W), tok(V_W), tok(2 * GLA_RANK)],
        out_shape=[jax.ShapeDtypeStruct((bsz, t_tot, QK_W), BF16),
                   jax.ShapeDtypeStruct((bsz, t_tot, QK_W), BF16),
                   jax.ShapeDtypeStruct((bsz, t_tot, V_W), BF16),
                   jax.ShapeDtypeStruct((bsz, t_tot, V_W), BF16),
                   jax.ShapeDtypeStruct((bsz, t_tot, 2 * GLA_RANK), F32)],
        compiler_params=_params(("parallel", "arbitrary")),
        name="gla_in",
    )(ctx, x, sc, sh, ng, w_in)


def _mlstm_in_kernel(ctx_ref, x_ref, sc_ref, sh_ref, ng_ref, w_ref, wgt_ref, bgt_ref, cw_ref, cb_ref,
                     q_ref, k_ref, v_ref, og_ref, gt_ref, pre_ref, last_ref, *, nblk):
    t = pl.program_id(1)

    @pl.when(t == 0)
    def _():
        last_ref[...] = jnp.zeros_like(last_ref)
        pre_ref[...] = jnp.zeros_like(pre_ref)

    h = _rms_mod(_load_tile(ctx_ref, x_ref), ng_ref[...], sc_ref[...], sh_ref[...]).astype(BF16)

    blk = t - 1
    x = pre_ref[(t + 1) % 2]
    pre = _dot(h, w_ref[:, 0:2 * QK_W])
    pre_ref[t % 2] = pre
    has_prev = blk > 1
    has_next = jnp.logical_and(blk > 0, blk != nblk - 1)
    prev = jnp.where(has_prev, last_ref[0:1, :], 0.0)
    nxt = jnp.where(has_next, pre[0:1, :], 0.0)
    row = lax.broadcasted_iota(jnp.int32, (BLK, 1), 0)
    x_m1 = jnp.where(row == 0, prev, pltpu.roll(x, 1, 0))
    x_p1 = jnp.where(row == BLK - 1, nxt, pltpu.roll(x, BLK - 1, 0))
    last_ref[0:1, :] = x[BLK - 1:BLK, :]

    def conv_piece(r0, r1):
        qk = (x_m1[r0:r1] * cw_ref[0:1, :] + x[r0:r1] * cw_ref[1:2, :] + x_p1[r0:r1] * cw_ref[2:3, :]
              + cb_ref[...])
        qk = qk * _sigmoid(qk)
        q_ref[r0:r1, :] = qk[:, :QK_W].astype(BF16)
        k_ref[r0:r1, :] = (qk[:, QK_W:] * D_K ** -0.5).astype(BF16)

    def gates():
        gt = _dot_nt(wgt_ref[...], h) + bgt_ref[...]
        is_forget = lax.broadcasted_iota(jnp.int32, gt.shape, 0) >= 2 * N_HEADS
        gt_ref[...] = jnp.where(is_forget, _log_sigmoid(gt), gt)

    n_pieces = 4
    step = BLK // n_pieces
    conv_piece(0, step)
    v_ref[...] = _dot(h, w_ref[:, 2 * QK_W:2 * QK_W + V_W]).astype(BF16)
    conv_piece(step, 2 * step)
    og_ref[...] = _dot(h, w_ref[:, 2 * QK_W + V_W:2 * QK_W + 2 * V_W]).astype(BF16)
    conv_piece(2 * step, 3 * step)
    gates()
    conv_piece(3 * step, BLK)


def _mlstm_in(ctx, x, sc, sh, ng, w_in, b_gate, conv_w, conv_b):
    bsz, seq, d = x.shape
    rows = seq // GRID_W
    assert rows == BLK
    nblk = 1 + GRID_W
    t_tot = nblk * BLK
    n_in = w_in.shape[1]
    n_g = 4 * N_HEADS
    xcol = x.reshape(bsz, rows, GRID_W * d)
    w_gt = w_in[:, 2 * QK_W + 2 * V_W:].T

    def late(width):
        return pl.BlockSpec((None, BLK, width), lambda b, t: (b, jnp.maximum(t - 1, 0), 0))

    def tok(width):
        return pl.BlockSpec((None, BLK, width), lambda b, t: (b, jnp.minimum(t, nblk - 1), 0))

    return pl.pallas_call(
        functools.partial(_mlstm_in_kernel, nblk=nblk),
        grid=(bsz, nblk + 1),
        in_specs=[pl.BlockSpec((None, BLK, d), lambda b, t: (b, 0, 0)),
                  pl.BlockSpec((None, rows, d), lambda b, t: (b, 0, jnp.clip(t - 1, 0, GRID_W - 1))),
                  _sel_spec(), _sel_spec(), _full((1, d)), _full((d, n_in)), _full((n_g, d)),
                  _full((n_g, 1)), _full((3, 2 * QK_W)), _full((1, 2 * QK_W))],
        out_specs=[late(QK_W), late(QK_W), tok(V_W), tok(V_W),
                   pl.BlockSpec((None, n_g, BLK), lambda b, t: (b, 0, jnp.minimum(t, nblk - 1)))],
        out_shape=[jax.ShapeDtypeStruct((bsz, t_tot, QK_W), BF16),
                   jax.ShapeDtypeStruct((bsz, t_tot, QK_W), BF16),
                   jax.ShapeDtypeStruct((bsz, t_tot, V_W), BF16),
                   jax.ShapeDtypeStruct((bsz, t_tot, V_W), BF16),
                   jax.ShapeDtypeStruct((bsz, n_g, t_tot), F32)],
        scratch_shapes=[pltpu.VMEM((2, BLK, 2 * QK_W), F32), pltpu.VMEM((8, 2 * QK_W), F32)],
        compiler_params=_params(("parallel", "arbitrary")),
        name="mlstm_in",
    )(ctx, xcol, sc, sh, ng, w_in, w_gt, b_gate.reshape(n_g, 1), conv_w, conv_b.reshape(1, 2 * QK_W))


def _block_tri(n, sub, lower):
    ri = lax.broadcasted_iota(jnp.int32, (n, n), 0)
    ci = lax.broadcasted_iota(jnp.int32, (n, n), 1)
    shift = sub.bit_length() - 1
    assert sub == 1 << shift
    same = (ri >> shift) == (ci >> shift)
    tri = (ci <= ri) if lower else (ci >= ri)
    return jnp.where(same & tri, 1.0, 0.0).astype(BF16)


def _tri_mask(n, lower):
    ri = lax.broadcasted_iota(jnp.int32, (n, n), 0)
    ci = lax.broadcasted_iota(jnp.int32, (n, n), 1)
    return (ci <= ri) if lower else (ci >= ri)


def _staggered_round_robin(groups, lag):
    live, rnd = [], 0
    while live or rnd <= lag * (len(groups) - 1):
        if rnd % lag == 0 and rnd // lag < len(groups):
            live = live + list(groups[rnd // lag])
        alive = []
        for gen in live:
            try:
                next(gen)
                alive.append(gen)
            except StopIteration:
                pass
        live, rnd = alive, rnd + 1


def _gla_scan_kernel(qf, kf, vf, zf, qb, kb, vb, zb, w2_ref, b2_ref, of_ref, ob_ref, *s_refs):
    @pl.when(pl.program_id(1) == 0)
    def _():
        for s_ref in s_refs:
            s_ref[...] = jnp.zeros_like(s_ref)

    dirs = ((qf, kf, vf, zf, of_ref), (qb, kb, vb, zb, ob_ref))
    tris = (_block_tri(BLK, SUB, lower=True), _block_tri(BLK, SUB, lower=False))
    masks = (_tri_mask(SUB, lower=True), _tri_mask(SUB, lower=False))

    def unit(d, h):
        q_ref, k_ref, v_ref, z_ref, o_ref = dirs[d]
        s_ref = s_refs[d * N_HEADS + h]
        fwd = d == 0
        ks = slice(h * D_K, (h + 1) * D_K)
        vs = slice(h * D_V, (h + 1) * D_V)
        za = z_ref[:, d * GLA_RANK:(d + 1) * GLA_RANK].astype(BF16)
        y = _dot(za, w2_ref[d, :, ks])
        yield
        la = _log_sigmoid(y + b2_ref[d, :, ks]) * (1.0 / GLA_TAU)
        hi, lo = _split_bf16(la)
        bcum = _dot(tris[d], jnp.concatenate([hi, lo], axis=1))
        yield
        bcum = bcum[:, :D_K] + bcum[:, D_K:]
        ops = {}
        for sub in (0, 1):
            rs = slice(sub * SUB, (sub + 1) * SUB)
            bs = bcum[rs]
            tot = bs[SUB - 1:SUB] if fwd else bs[0:1]
            mid = 0.5 * tot
            e_mid = jnp.exp(mid)
            q_mid = q_ref[rs, ks].astype(F32) * jnp.exp(bs - mid)
            k_mid = k_ref[rs, ks].astype(F32) * jnp.exp(mid - bs)
            q_in = (q_mid * e_mid).astype(BF16)
            k_st_t = (k_mid * e_mid).T.astype(BF16)
            ops[sub] = (q_mid.astype(BF16), q_in, k_mid.astype(BF16), k_st_t, e_mid * e_mid)
            yield
        for sub in ((0, 1) if fwd else (1, 0)):
            rs = slice(sub * SUB, (sub + 1) * SUB)
            q_mid, q_in, k_mid, k_st_t, decay = ops[sub]
            scores = _dot_nt(q_mid, k_mid)
            yield
            p = jnp.where(masks[d], scores, 0.0).astype(BF16)
            vh = v_ref[rs, vs]
            s_old = s_ref[...]
            o = _dot(jnp.concatenate([p, q_in], axis=1), jnp.concatenate([vh, s_old.astype(BF16)], axis=0))
            upd = _dot(k_st_t, vh)
            yield
            o_ref[rs, vs] = o.astype(o_ref.dtype)
            dcol = jnp.broadcast_to(decay, (D_K, D_K)).T
            s_ref[...] = s_old * jnp.concatenate([dcol, dcol], axis=1) + upd
            yield

    _staggered_round_robin([[unit(d, h) for h in hs for d in (0, 1)] for hs in ((0, 1), (2, 3))], lag=3)


def _scan_maps(nblk):
    def fwd(b, s):
        return (b, s, 0)

    def bwd(b, s):
        return (b, jnp.where(s == 0, 0, nblk - s), 0)

    return fwd, bwd


def _gla_scan(q, k, v, za, w_a2, b_a2):
    bsz, t_tot, _ = q.shape
    nblk = t_tot // BLK
    fwd, bwd = _scan_maps(nblk)

    def specs(m):
        return [pl.BlockSpec((None, BLK, QK_W), m), pl.BlockSpec((None, BLK, QK_W), m),
                pl.BlockSpec((None, BLK, V_W), m), pl.BlockSpec((None, BLK, 2 * GLA_RANK), m)]

    return pl.pallas_call(
        _gla_scan_kernel,
        grid=(bsz, nblk),
        in_specs=specs(fwd) + specs(bwd) + [_full((2, GLA_RANK, QK_W)), _full((2, 1, QK_W))],
        out_specs=[pl.BlockSpec((None, BLK, V_W), fwd), pl.BlockSpec((None, BLK, V_W), bwd)],
        out_shape=[jax.ShapeDtypeStruct((bsz, t_tot, V_W), BF16)] * 2,
        scratch_shapes=[pltpu.VMEM((D_K, D_V), F32)] * (2 * N_HEADS),
        compiler_params=_params(("parallel", "arbitrary")),
        name="gla_scan",
    )(q, k, v, za, q, k, v, za, w_a2.astype(BF16), b_a2.reshape(2, 1, QK_W))


def _mlstm_scan_kernel(qf, kf, vf, gtf, qb, kb, vb, gtb, of_ref, ob_ref, *state_refs):
    n_units = 2 * N_HEADS
    c_refs, nb_refs, m_refs = (state_refs[i * n_units:(i + 1) * n_units] for i in range(3))

    @pl.when(pl.program_id(1) == 0)
    def _():
        for ref in state_refs:
            ref[...] = jnp.zeros_like(ref)

    ones = jnp.ones((SUB, D_K), BF16)
    dirs = ((qf, kf, vf, gtf, of_ref), (qb, kb, vb, gtb, ob_ref))
    gates = []
    for d, (_, _, _, gt_ref, _) in enumerate(dirs):
        gt = gt_ref[...]
        gt_hi, gt_lo = _split_bf16(gt)
        tri_r = _block_tri(BLK, SUB, lower=d != 0)
        gates.append((gt, _dot(gt_hi, tri_r) + _dot(gt_lo, tri_r)))
    masks = (_tri_mask(SUB, lower=True), _tri_mask(SUB, lower=False))
    def unit(d, h):
        q_ref, k_ref, v_ref, _, o_ref = dirs[d]
        fwd = d == 0
        gt, f_row = gates[d]
        ks = slice(h * D_K, (h + 1) * D_K)
        vs = slice(h * D_V, (h + 1) * D_V)
        gi = d * N_HEADS + h
        gf_ = 2 * N_HEADS + gi
        c_ref, nb_ref, m_ref = c_refs[gi], nb_refs[gi], m_refs[gi]
        pre = {}
        for sub in (0, 1):
            rs = slice(sub * SUB, (sub + 1) * SUB)
            fc_r = f_row[gf_:gf_ + 1, rs]
            ic_r = gt[gi:gi + 1, rs]
            fc_b = jnp.broadcast_to(fc_r, (SUB, SUB)).T
            d_mat = jnp.where(masks[d], fc_b + (ic_r - fc_r), -jnp.inf)
            d_max = jnp.max(d_mat, axis=-1, keepdims=True)
            qk = _dot_nt(q_ref[rs, ks], k_ref[rs, ks])
            k_t = k_ref[rs, ks].astype(F32).T
            pre[sub] = (fc_r, ic_r, fc_b, d_mat, d_max, qk, k_t)
            yield
        for sub in ((0, 1) if fwd else (1, 0)):
            rs = slice(sub * SUB, (sub + 1) * SUB)
            fc_r, ic_r, fc_b, d_mat, d_max, qk, k_t = pre[sub]
            vh = v_ref[rs, vs]
            c_old = c_ref[...]
            n_old = nb_ref[...]
            m_old = m_ref[...]
            inter = fc_b + m_old
            m_i = jnp.maximum(inter, d_max)
            w_inter = jnp.exp(inter - m_i)
            s_mat = qk * jnp.exp(d_mat - m_i)
            lhs = jnp.concatenate([s_mat.astype(BF16),
                                   (w_inter * q_ref[rs, ks].astype(F32)).astype(BF16)], axis=1)
            num = _dot(lhs, jnp.concatenate([vh, c_old.astype(BF16)], axis=0))
            den = _dot(lhs, jnp.concatenate([ones, n_old.astype(BF16)], axis=0))
            f_last = fc_r[:, SUB - 1:SUB] if fwd else fc_r[:, 0:1]
            g_r = f_last - fc_r + ic_r
            m_new = jnp.maximum(f_last + m_old, jnp.max(g_r, axis=-1, keepdims=True))
            dec = jnp.exp(f_last + m_old - m_new)[:, 0:1]
            kw_t = (k_t * jnp.exp(g_r - m_new)).astype(BF16)
            c_upd = _dot(kw_t, vh)
            n_upd = _dot(kw_t, ones)
            yield
            r = 1.0 / jnp.maximum(jnp.abs(den), jnp.exp(-m_i))
            o_ref[rs, vs] = (num * jnp.concatenate([r, r], axis=1)).astype(o_ref.dtype)
            c_ref[...] = dec * c_old + c_upd
            nb_ref[...] = dec * n_old + n_upd
            m_ref[...] = m_new
            yield

    _staggered_round_robin([[unit(d, h) for h in hs for d in (0, 1)] for hs in ((0, 1), (2, 3))], lag=2)


def _mlstm_scan(q, k, v, gt):
    bsz, t_tot, _ = v.shape
    nblk = t_tot // BLK
    fwd, bwd = _scan_maps(nblk)
    n_g = 4 * N_HEADS

    def specs(m):
        return [pl.BlockSpec((None, BLK, QK_W), m), pl.BlockSpec((None, BLK, QK_W), m),
                pl.BlockSpec((None, BLK, V_W), m),
                pl.BlockSpec((None, n_g, BLK), lambda b, s: (b, 0, m(b, s)[1]))]

    return pl.pallas_call(
        _mlstm_scan_kernel,
        grid=(bsz, nblk),
        in_specs=specs(fwd) + specs(bwd),
        out_specs=[pl.BlockSpec((None, BLK, V_W), fwd), pl.BlockSpec((None, BLK, V_W), bwd)],
        out_shape=[jax.ShapeDtypeStruct((bsz, t_tot, V_W), BF16)] * 2,
        scratch_shapes=([pltpu.VMEM((D_K, D_V), F32)] * (2 * N_HEADS)
                        + [pltpu.VMEM((D_K, D_K), F32)] * (2 * N_HEADS)
                        + [pltpu.VMEM((1, D_K), F32)] * (2 * N_HEADS)),
        compiler_params=_params(("parallel", "arbitrary")),
        name="mlstm_scan",
    )(q, k, v, gt, q, k, v, gt)


OUT_PIECES = 2


def _head_norm_gate(of_ref, ob_ref, gate, ng_ref, rows):
    o = of_ref[rows, :].astype(F32) + ob_ref[rows, :].astype(F32)
    parts = []
    for h in range(N_HEADS):
        oh = o[:, h * D_V:(h + 1) * D_V]
        ms = jnp.mean(oh * oh, axis=-1, keepdims=True)
        parts.append(oh * lax.rsqrt(ms + EPS))
    return (jnp.concatenate(parts, axis=1) * ng_ref[...] * gate).astype(BF16)


def _mixer_out_pieces(gated, w_ref, finish):
    pr = BLK // OUT_PIECES
    rows = [slice(j * pr, (j + 1) * pr) for j in range(OUT_PIECES)]
    nxt = gated(rows[0])
    for j in range(OUT_PIECES):
        y = _dot(nxt, w_ref[...])
        if j + 1 < OUT_PIECES:
            nxt = gated(rows[j + 1])
        finish(rows[j], y)


def _gla_out_kernel(of_ref, ob_ref, r_ref, ctx_ref, g1_ref, ng_ref, w_ref, ctx_out, delta_out):
    t = pl.program_id(1)

    def gated(rows):
        r = r_ref[rows, :].astype(F32)
        return _head_norm_gate(of_ref, ob_ref, r * _sigmoid(r), ng_ref, rows)

    ys = []

    def finish(rows, y):
        delta_out[rows, :] = (g1_ref[...] * y).astype(delta_out.dtype)
        ys.append((rows, y))

    _mixer_out_pieces(gated, w_ref, finish)

    @pl.when(t == 0)
    def _():
        for rows, y in ys:
            ctx_out[rows, :] = ctx_ref[rows, :] + g1_ref[...] * y


def _gla_out(o_f, o_b, r, ctx, g1, ng, w_out):
    bsz, t_tot, _ = o_f.shape
    d = ctx.shape[2]
    nblk = t_tot // BLK
    tok = pl.BlockSpec((None, BLK, V_W), lambda b, t: (b, t, 0))
    ctx_spec = pl.BlockSpec((None, BLK, d), lambda b, t: (b, 0, 0))
    lat_spec = pl.BlockSpec((None, BLK, d), lambda b, t: (b, jnp.maximum(t - 1, 0), 0))
    return pl.pallas_call(
        _gla_out_kernel,
        grid=(bsz, nblk),
        in_specs=[tok, tok, tok, ctx_spec, _sel_spec(), _full((1, V_W)), _full((V_W, d))],
        out_specs=[ctx_spec, lat_spec],
        out_shape=[jax.ShapeDtypeStruct(ctx.shape, F32),
                   jax.ShapeDtypeStruct((bsz, t_tot - BLK, d), BF16)],
        compiler_params=_params(("parallel", "arbitrary")),
        name="gla_out",
    )(o_f, o_b, r, ctx, g1, ng, w_out)


MLSTM_OUT_COLS = 4


def _mlstm_out_kernel(of_ref, ob_ref, og_ref, g1_ref, ng_ref, w_ref, delta_out):
    d = w_ref.shape[1]
    of_ref, ob_ref, og_ref = of_ref.at[0], ob_ref.at[0], og_ref.at[0]

    def gated(j):
        rows = slice(j * BLK, (j + 1) * BLK)
        return _head_norm_gate(of_ref, ob_ref, _sigmoid(og_ref[rows, :].astype(F32)), ng_ref, rows)

    nxt = gated(0)
    for j in range(MLSTM_OUT_COLS):
        y = _dot(nxt, w_ref[...])
        if j + 1 < MLSTM_OUT_COLS:
            nxt = gated(j + 1)
        delta_out[:, j * d:(j + 1) * d] = (g1_ref[...] * y).astype(delta_out.dtype)


def _mlstm_out(h_f, h_b, og, g1, ng, w_out):
    bsz, t_tot, _ = h_f.shape
    d = w_out.shape[1]
    seq = t_tot - BLK
    rows = seq // GRID_W
    cols = MLSTM_OUT_COLS
    tok = pl.BlockSpec((pl.Element(1), pl.Element(cols * BLK), pl.Element(V_W)),
                       lambda b, c: (b, pl.multiple_of(BLK + cols * BLK * c, BLK), 0))
    out = pl.pallas_call(
        _mlstm_out_kernel,
        grid=(bsz, GRID_W // cols),
        in_specs=[tok, tok, tok,
                  pl.BlockSpec((None, 1, d), lambda b, c: (b, 0, 0)),
                  _full((1, V_W)), _full((V_W, d))],
        out_specs=pl.BlockSpec((None, rows, cols * d), lambda b, c: (b, 0, c)),
        out_shape=jax.ShapeDtypeStruct((bsz, rows, GRID_W * d), BF16),
        compiler_params=_params(("parallel", "arbitrary")),
        name="mlstm_out",
    )(h_f, h_b, og, g1, ng, w_out)
    return out.reshape(bsz, seq, d)


GELU_K0 = 0.7978845608028654
GELU_K1 = GELU_K0 * 0.044715


def _ffn_kernel(*refs, width, tm, vertical, has_delta, final_norm):
    refs = list(refs)
    x_ref = refs.pop(0)
    xu_ref, xd_ref = (refs.pop(0), refs.pop(0)) if vertical else (None, None)
    d_ref, du_ref, dd_ref = (refs.pop(0), refs.pop(0), refs.pop(0)) if has_delta else (None,) * 3
    sc_ref, sh_ref, g2_ref, ng_ref, wa_ref, wg_ref, cw_ref, cb_ref, wd_ref = refs[:9]
    refs = refs[9:]
    fg_ref = refs.pop(0) if final_norm else None
    out_ref, hext_ref, act_ref = refs

    def residual(x_r, delta_r):
        return x_r[...] + delta_r[...].astype(F32) if has_delta else x_r[...]

    t = pl.program_id(1)
    nt = pl.num_programs(1)
    halo = width if vertical else 0
    ext = tm + 2 * halo

    def hmod(xt):
        return _rms_mod(xt, ng_ref[...], sc_ref[...], sh_ref[...]).astype(BF16)

    assert width & (width - 1) == 0
    pr = max(width, min(tm, FFN_PIECE_ROWS))
    assert pr % width == 0 and tm % pr == 0
    n_pieces = tm // pr
    wpos = lax.broadcasted_iota(jnp.int32, (pr, FF_CHUNK), 0) & (width - 1)
    first_col = wpos == 0
    last_col = wpos == width - 1
    taps = (0, 1, 2) if vertical else (1,)

    def chunk(c):
        return slice(c * FF_CHUNK, (c + 1) * FF_CHUNK)

    def activation(c, a, g, r0):
        cw = cw_ref[:, chunk(c)]
        u = []
        for dw in range(3):
            s = None
            for dr in taps:
                term = g[dr * halo + r0:dr * halo + r0 + pr] * cw[3 * dr + dw:3 * dr + dw + 1]
                s = term if s is None else s + term
            u.append(s)
        conv = ((u[1] + cb_ref[:, chunk(c)]) + jnp.where(first_col, 0.0, pltpu.roll(u[0], 1, 0))
                + jnp.where(last_col, 0.0, pltpu.roll(u[2], pr - 1, 0)))
        th = jnp.tanh(conv * (GELU_K0 + GELU_K1 * (conv * conv)))
        return ((conv + conv * th) * a[r0:r0 + pr]).astype(BF16)

    def up_a(c):
        return _dot(hext_ref[halo:halo + tm, :], wa_ref[:, chunk(c)])

    def up_g(c):
        return _dot(hext_ref[...], wg_ref[:, chunk(c)])

    out_ref[...] = residual(x_ref, d_ref)
    hext_ref[halo:halo + tm, :] = hmod(out_ref[...])
    if vertical:
        zeros = jnp.zeros((halo, hext_ref.shape[1]), BF16)
        hext_ref[0:halo, :] = jnp.where(t == 0, zeros, hmod(residual(xu_ref, du_ref)))
        hext_ref[halo + tm:ext, :] = jnp.where(t == nt - 1, zeros, hmod(residual(xd_ref, dd_ref)))

    state = {"a": up_a(0), "g": up_g(0)}
    for c in range(N_FF_CHUNKS):
        a, g = state["a"], state["g"]
        matmuls = []
        if c + 1 < N_FF_CHUNKS:
            matmuls.append(lambda c=c: state.__setitem__("a", up_a(c + 1)))
            matmuls.append(lambda c=c: state.__setitem__("g", up_g(c + 1)))
        for j in range(n_pieces):
            if matmuls:
                matmuls.pop(0)()
            act_ref[j * pr:(j + 1) * pr, chunk(c)] = activation(c, a, g, j * pr)
        for m in matmuls:
            m()
    y = out_ref[...] + g2_ref[...] * _dot(act_ref[...], wd_ref[...])
    if final_norm:
        ms = jnp.mean(y * y, axis=-1, keepdims=True)
        y = y * lax.rsqrt(ms + EPS) * fg_ref[...]
    out_ref[...] = y


def _conv_ffn(x, delta, sc, sh, g2, ng, wa, wg, cw, cb, wd, *, width, tm, vertical, final_g=None):
    bsz, seq, d = x.shape
    nt = seq // tm
    vec = pl.BlockSpec((None, 1, d), lambda b, t: (b, 0, 0))
    tile = pl.BlockSpec((None, tm, d), lambda b, t: (b, t, 0))
    halo_specs = []
    if vertical:
        rpt = tm // width
        nrows = seq // width
        halo_specs = [pl.BlockSpec((None, width, d), lambda b, t: (b, jnp.maximum(t * rpt - 1, 0), 0)),
                      pl.BlockSpec((None, width, d), lambda b, t: (b, jnp.minimum((t + 1) * rpt, nrows - 1), 0))]
    in_specs = [tile] + halo_specs
    args = [x] * (1 + len(halo_specs))
    if delta is not None:
        assert vertical
        in_specs += [tile] + halo_specs
        args += [delta] * 3
    in_specs += [vec, vec, vec, _full((1, d)), _resident(wa.shape), _resident(wg.shape),
                 _resident(cw.shape), _resident(cb.shape), _resident(wd.shape)]
    args += [sc, sh, g2, ng, wa, wg, cw, cb, wd]
    if final_g is not None:
        in_specs.append(_full((1, d)))
        args.append(final_g)
    ext = tm + (2 * width if vertical else 0)
    return pl.pallas_call(
        functools.partial(_ffn_kernel, width=width, tm=tm, vertical=vertical,
                          has_delta=delta is not None, final_norm=final_g is not None),
        grid=(bsz, nt),
        in_specs=in_specs,
        out_specs=tile,
        out_shape=jax.ShapeDtypeStruct(x.shape, F32),
        scratch_shapes=[pltpu.VMEM((ext, d), BF16), pltpu.VMEM((tm, wd.shape[0]), BF16)],
        compiler_params=_params(("parallel", "arbitrary")),
        name="conv_ffn_grid" if vertical else "conv_ffn_seq",
    )(*args)


def _ffn_weights(w_up, conv_w, conv_b, w_down):
    d = w_up.shape[0]

    assert w_up.shape == (d, 2 * D_FF)
    wa = w_up[:, :D_FF].astype(BF16)
    wg = w_up[:, D_FF:].astype(BF16)
    cw = conv_w.reshape(9, D_FF)
    cb = conv_b.reshape(1, D_FF)
    wd = (0.5 * w_down).astype(BF16)
    return wa, wg, cw, cb, wd


def kernel(x, c, ctx, c_ctx, ada_w, ada_b, norm_mix_g, norm_ffn_g, gla_w_in, gla_w_a2, gla_b_a2, gla_norm_g, gla_w_out, mlstm_w_in, mlstm_b_gate, mlstm_conv_w, mlstm_conv_b, mlstm_norm_g, mlstm_w_out, ffn_w_up, ffn_conv_w, ffn_conv_b, ffn_w_down, final_norm_g):
    bsz, seq, d = x.shape
    depth = ada_w.shape[0]
    assert bsz == 2 and d == D_MODEL and ctx.shape[1] == BLK and seq == BLK * GRID_W and depth == 2

    cvec = jnp.zeros((8, d), F32).at[:bsz].set(c).at[bsz].set(c_ctx)
    mods = _modulation(cvec, ada_w, ada_b)

    def mod(i, j):
        return mods[i, :3, j * d:(j + 1) * d].reshape(3, 1, d)

    def row(v):
        return v.reshape(1, -1)

    sh1, sc1, g1, sh2, sc2, g2 = (mod(0, j) for j in range(N_MOD))
    nsh1, nsc1, ng1, nsh2, nsc2, ng2 = (mod(1, j) for j in range(N_MOD))
    q, k, v, r, za = _gla_in(ctx, x, sc1, sh1, row(norm_mix_g[0]), gla_w_in[0].astype(BF16))
    o_f, o_b = _gla_scan(q, k, v, za, gla_w_a2[0], gla_b_a2[0])
    ctx, delta = _gla_out(o_f, o_b, r, ctx, g1, row(gla_norm_g[0]), gla_w_out[0].astype(BF16))
    ffn_w = _ffn_weights(ffn_w_up[0], ffn_conv_w[0], ffn_conv_b[0], ffn_w_down[0])
    x = _conv_ffn(x, delta, sc2[:bsz], sh2[:bsz], g2[:bsz], row(norm_ffn_g[0]), *ffn_w,
                  width=GRID_W, tm=FFN_ROWS * GRID_W, vertical=True)
    ctx = _conv_ffn(ctx, None, jnp.broadcast_to(sc2[2:], (bsz, 1, d)), jnp.broadcast_to(sh2[2:], (bsz, 1, d)),
                    jnp.broadcast_to(g2[2:], (bsz, 1, d)), row(norm_ffn_g[0]), *ffn_w,
                    width=BLK, tm=BLK, vertical=False)

    q, k, v, og, gt = _mlstm_in(ctx, x, nsc1, nsh1, row(norm_mix_g[1]), mlstm_w_in[0].astype(BF16),
                                mlstm_b_gate[0], mlstm_conv_w[0], mlstm_conv_b[0])
    h_f, h_b = _mlstm_scan(q, k, v, gt)
    delta = _mlstm_out(h_f, h_b, og, ng1[:bsz], row(mlstm_norm_g[0]), mlstm_w_out[0].astype(BF16))
    ffn_w = _ffn_weights(ffn_w_up[1], ffn_conv_w[1], ffn_conv_b[1], ffn_w_down[1])
    return _conv_ffn(x, delta, nsc2[:bsz], nsh2[:bsz], ng2[:bsz], row(norm_ffn_g[1]), *ffn_w,
                     width=GRID_W, tm=FFN_ROWS * GRID_W, vertical=True, final_g=row(final_norm_g))
```
